```python
import jax, jax.numpy as jnp
from jax import lax
import numpy as np

D_MODEL = 1024
BATCH = 32
SEQ = 2048
DEPTH = 4

CHUNK = 64
EPS = 1e-6
Q_BLOCK = 128

POOL_WINDOWS = (2, 4, 8, 16)
POOL_GROUPS = len(POOL_WINDOWS)
POOL_GROUP_DIM = D_MODEL // 16
POOL_WIDTH = POOL_GROUPS * POOL_GROUP_DIM
SB_HEADS = 8
SB_HEAD_DIM = D_MODEL // 16
SB_WIDTH = SB_HEADS * SB_HEAD_DIM
CONV_K = 3
CONV_WIDTH = D_MODEL // 4
N_BRANCH = 3
D_FF = ((8 * D_MODEL // 3 + 255) // 256) * 256

OFF_POOL = 0
OFF_Q = OFF_POOL + POOL_WIDTH
OFF_K = OFF_Q + SB_WIDTH
OFF_V = OFF_K + SB_WIDTH
OFF_CX = OFF_V + SB_WIDTH
OFF_CB = OFF_CX + CONV_WIDTH
OFF_CC = OFF_CB + CONV_WIDTH
OFF_GATE = OFF_CC + CONV_WIDTH
IN_WIDTH = OFF_GATE + N_BRANCH * D_MODEL

kernel_name = "hybrid_pool_stickbreak_shortconv_macaron"


def _rmsnorm(x, g):
    x32 = x.astype(jnp.float32)
    y = x32 * lax.rsqrt(jnp.mean(x32 * x32, axis=-1, keepdims=True) + EPS)
    return (y * g.astype(jnp.float32)).astype(x.dtype)


def _swiglu(h, w_gate, w_up, w_down):
    return (jax.nn.silu(h @ w_gate) * (h @ w_up)) @ w_down


def _pool_mixer(u, w_mix, scale):
    b, s, _ = u.shape
    ug = u.reshape(b, s, POOL_GROUPS, POOL_GROUP_DIM)
    count = jnp.arange(1, s + 1, dtype=jnp.float32)
    outs = []
    for g, w in enumerate(POOL_WINDOWS):
        ui = ug[:, :, g].astype(jnp.float32)
        cs = jnp.cumsum(ui, axis=1)
        lag = jnp.pad(cs, ((0, 0), (w, 0), (0, 0)))[:, :s]
        mean = (cs - lag) / jnp.minimum(count, float(w))[None, :, None]
        outs.append(mean - ui)
    pooled = jnp.stack(outs, axis=2).astype(u.dtype)
    mixed = jnp.einsum('bsgc,gcd->bsgd', pooled, w_mix).reshape(b, s, POOL_WIDTH)
    return mixed * scale


def _stick_breaking(q, k, v):
    s = q.shape[1]
    scale = SB_HEAD_DIM ** -0.5
    outs = []
    for i in range(s // Q_BLOCK):
        q0 = i * Q_BLOCK
        klen = q0 + Q_BLOCK
        qb = q[:, q0:klen]
        kb = k[:, :klen]
        vb = v[:, :klen]
        z = jnp.einsum('bqhd,bkhd->bhqk', qb, kb).astype(jnp.float32) * scale
        qpos = q0 + jnp.arange(Q_BLOCK)
        kpos = jnp.arange(klen)
        mask = kpos[None, :] < qpos[:, None]
        log_beta = jax.nn.log_sigmoid(z)
        log_keep = jnp.where(mask, jax.nn.log_sigmoid(-z), 0.0)
        between = lax.cumsum(log_keep, axis=3, reverse=True) - log_keep
        wts = jnp.where(mask, jnp.exp(log_beta + between), 0.0)
        outs.append(jnp.einsum('bhqk,bkhd->bqhd', wts.astype(v.dtype), vb))
    return jnp.concatenate(outs, axis=1)


def _short_conv(xc, gb, gc, conv_w, conv_b):
    s = xc.shape[1]
    u = gc * xc
    up = jnp.pad(u, ((0, 0), (CONV_K - 1, 0), (0, 0)))
    y = conv_b + sum(conv_w[j] * up[:, j:j + s] for j in range(CONV_K))
    return gb * y


def _mixing(h, w_in, b_gate, pool_w, pool_scale, conv_w, conv_b,
            w_br_pool, w_br_sb, w_br_conv, w_out):
    b, s, _ = h.shape
    p = h @ w_in
    a_out = _pool_mixer(p[..., OFF_POOL:OFF_Q], pool_w, pool_scale)
    q = p[..., OFF_Q:OFF_K].reshape(b, s, SB_HEADS, SB_HEAD_DIM)
    k = p[..., OFF_K:OFF_V].reshape(b, s, SB_HEADS, SB_HEAD_DIM)
    v = p[..., OFF_V:OFF_CX].reshape(b, s, SB_HEADS, SB_HEAD_DIM)
    b_out = _stick_breaking(q, k, v).reshape(b, s, SB_WIDTH)
    c_out = _short_conv(p[..., OFF_CX:OFF_CB], p[..., OFF_CB:OFF_CC], p[..., OFF_CC:OFF_GATE],
                        conv_w, conv_b)
    gates = jax.nn.sigmoid(p[..., OFF_GATE:].reshape(b, s, N_BRANCH, D_MODEL) + b_gate)
    merged = (gates[:, :, 0] * (a_out @ w_br_pool)
              + gates[:, :, 1] * (b_out @ w_br_sb)
              + gates[:, :, 2] * (c_out @ w_br_conv))
    return merged @ w_out


def setup_inputs(seed: int = 0) -> dict:
    key = jax.random.key(seed)
    ks = iter(jax.random.split(key, 32))
    f32 = jnp.float32

    def nrm(shape, fan_in):
        return jax.random.normal(next(ks), shape, f32) * (fan_in ** -0.5)

    def gain(shape):
        return 1.0 + 0.05 * jax.random.normal(next(ks), shape, f32)

    def small(shape):
        return 0.01 * jax.random.normal(next(ks), shape, f32)

    L, D = DEPTH, D_MODEL
    return {
        "x": jax.random.normal(next(ks), (BATCH, SEQ, D), f32),
        "ffn1_pre_g": gain((L, D)),
        "ffn1_post_g": gain((L, D)),
        "ffn1_w_gate": nrm((L, D, D_FF), D),
        "ffn1_w_up": nrm((L, D, D_FF), D),
        "ffn1_w_down": nrm((L, D_FF, D), D_FF),
        "mix_pre_g": gain((L, D)),
        "mix_post_g": gain((L, D)),
        "w_in": nrm((L, D, IN_WIDTH), D),
        "b_gate": small((L, N_BRANCH, D)),
        "pool_w": nrm((L, POOL_GROUPS, POOL_GROUP_DIM, POOL_GROUP_DIM), POOL_GROUP_DIM),
        "pool_scale": gain((L, POOL_WIDTH)),
        "conv_w": nrm((L, CONV_K, CONV_WIDTH), CONV_K),
        "conv_b": small((L, CONV_WIDTH)),
        "w_br_pool": nrm((L, POOL_WIDTH, D), POOL_WIDTH),
        "w_br_sb": nrm((L, SB_WIDTH, D), SB_WIDTH),
        "w_br_conv": nrm((L, CONV_WIDTH, D), CONV_WIDTH),
        "w_out": nrm((L, D, D), D),
        "ffn2_pre_g": gain((L, D)),
        "ffn2_post_g": gain((L, D)),
        "ffn2_w_gate": nrm((L, D, D_FF), D),
        "ffn2_w_up": nrm((L, D, D_FF), D),
        "ffn2_w_down": nrm((L, D_FF, D), D_FF),
    }


def reference(x, ffn1_pre_g, ffn1_post_g, ffn1_w_gate, ffn1_w_up, ffn1_w_down,
              mix_pre_g, mix_post_g, w_in, b_gate, pool_w, pool_scale, conv_w, conv_b,
              w_br_pool, w_br_sb, w_br_conv, w_out,
              ffn2_pre_g, ffn2_post_g, ffn2_w_gate, ffn2_w_up, ffn2_w_down):
    for l in range(DEPTH):
        h = _swiglu(_rmsnorm(x, ffn1_pre_g[l]), ffn1_w_gate[l], ffn1_w_up[l], ffn1_w_down[l])
        x = x + 0.5 * _rmsnorm(h, ffn1_post_g[l])
        h = _mixing(_rmsnorm(x, mix_pre_g[l]), w_in[l], b_gate[l], pool_w[l], pool_scale[l],
                    conv_w[l], conv_b[l], w_br_pool[l], w_br_sb[l], w_br_conv[l], w_out[l])
        x = x + _rmsnorm(h, mix_post_g[l])
        h = _swiglu(_rmsnorm(x, ffn2_pre_g[l]), ffn2_w_gate[l], ffn2_w_up[l], ffn2_w_down[l])
        x = x + 0.5 * _rmsnorm(h, ffn2_post_g[l])
    return x
```

```python
import functools

import jax
import jax.numpy as jnp
from jax import lax
from jax.experimental import pallas as pl
from jax.experimental.pallas import tpu as pltpu

F32 = jnp.float32
BF16 = jnp.bfloat16

EPS = 1e-6
D_MODEL = 1024
D_FF = 2816
FF_CHUNK = 256
POOL_WINDOWS = (2, 4, 8, 16)
POOL_GROUP_DIM = 64
POOL_WIDTH = 256
SB_HEADS = 8
SB_HEAD_DIM = 64
SB_WIDTH = 512
CONV_WIDTH = 256
N_BRANCH = 3
OFF_Q = POOL_WIDTH
OFF_CX = OFF_Q + 3 * SB_WIDTH
OFF_GATE = OFF_CX + 3 * CONV_WIDTH
LANES = 128
POOL_HALO = 16
CONV_HALO = 8

FFN_ROWS = 512
SEQ_TILE = 256
ATT_BLOCK = 256
VMEM_LIMIT = 56 * 1024 * 1024


def _rmsnorm(x, g):
    ms = jnp.mean(x * x, axis=-1, keepdims=True)
    return x * lax.rsqrt(ms + EPS) * g


def _sigmoid(x):
    return 1.0 / (1.0 + jnp.exp(-x))


def _dot(a, b):
    return jnp.dot(a, b, preferred_element_type=F32)


def _resident(shape):
    nd = len(shape)
    return pl.BlockSpec(shape, lambda *_: (0,) * nd, pipeline_mode=pl.Buffered(1))


def _ffn_kernel(x_ref, pre_ref, post_ref, wg_ref, wu_ref, wd_ref, o_ref, a_scr):
    x = x_ref[...]
    h = _rmsnorm(x, pre_ref[...]).astype(BF16)
    for c in range(D_FF // FF_CHUNK):
        cols = slice(c * FF_CHUNK, (c + 1) * FF_CHUNK)
        g = _dot(h, wg_ref[:, cols])
        u = _dot(h, wu_ref[:, cols])
        a_scr[:, cols] = ((g * _sigmoid(g)) * u).astype(BF16)
    y = _dot(a_scr[...], wd_ref[...])
    o_ref[...] = x + 0.5 * _rmsnorm(y, post_ref[...])


def _ffn(x2d, pre_g, post_g, wg, wu, wd):
    n, d = x2d.shape
    rows = min(FFN_ROWS, n)
    assert n % rows == 0
    return pl.pallas_call(
        _ffn_kernel,
        grid=(n // rows,),
        in_specs=[
            pl.BlockSpec((rows, d), lambda i: (i, 0)),
            _resident((1, d)), _resident((1, d)),
            _resident(wg.shape), _resident(wu.shape), _resident(wd.shape),
        ],
        out_specs=pl.BlockSpec((rows, d), lambda i: (i, 0)),
        out_shape=jax.ShapeDtypeStruct((n, d), F32),
        scratch_shapes=[pltpu.VMEM((rows, D_FF), BF16)],
        compiler_params=pltpu.CompilerParams(
            dimension_semantics=("arbitrary",), vmem_limit_bytes=VMEM_LIMIT),
        name="ffn",
    )(x2d, pre_g, post_g, wg, wu, wd)


def _qkv_kernel(x_ref, g_ref, w_ref, q_ref, kt_ref, v_ref):
    h = _rmsnorm(x_ref[0], g_ref[...]).astype(BF16)
    qkv = _dot(h, w_ref[...])
    q_ref[0] = (qkv[:, :SB_WIDTH] * (SB_HEAD_DIM ** -0.5)).astype(BF16)
    kt_ref[0, 0] = qkv[:, SB_WIDTH:2 * SB_WIDTH].T.astype(BF16)
    v_ref[0] = qkv[:, 2 * SB_WIDTH:].astype(BF16)


def _qkv(x, g, w):
    b, s, d = x.shape
    t = ATT_BLOCK
    assert s % t == 0
    return pl.pallas_call(
        _qkv_kernel,
        grid=(b, s // t),
        in_specs=[
            pl.BlockSpec((1, t, d), lambda bi, i: (bi, i, 0)),
            _resident((1, d)), _resident(w.shape),
        ],
        out_specs=[
            pl.BlockSpec((1, t, SB_WIDTH), lambda bi, i: (bi, i, 0)),
            pl.BlockSpec((1, 1, SB_WIDTH, t), lambda bi, i: (bi, i, 0, 0)),
            pl.BlockSpec((1, t, SB_WIDTH), lambda bi, i: (bi, i, 0)),
        ],
        out_shape=[
            jax.ShapeDtypeStruct((b, s, SB_WIDTH), BF16),
            jax.ShapeDtypeStruct((b, s // t, SB_WIDTH, t), BF16),
            jax.ShapeDtypeStruct((b, s, SB_WIDTH), BF16),
        ],
        compiler_params=pltpu.CompilerParams(
            dimension_semantics=("arbitrary", "arbitrary"), vmem_limit_bytes=VMEM_LIMIT),
        name="qkv",
    )(x, g, w)


def _attn_kernel(q_ref, kt_ref, v_ref, o_ref):
    blk = ATT_BLOCK
    i = pl.program_id(1)
    row = lax.broadcasted_iota(jnp.int32, (blk, blk), 0)
    col = lax.broadcasted_iota(jnp.int32, (blk, blk), 1)
    lower = row > col
    tri = jnp.where(lower, 1.0, 0.0).astype(BF16)
    lane = lax.broadcasted_iota(jnp.int32, (1, LANES), 1)
    head_lanes = (lane < SB_HEAD_DIM, lane >= SB_HEAD_DIM)

    for hp in range(SB_WIDTH // LANES):
        lanes = slice(hp * LANES, (hp + 1) * LANES)
        qp = q_ref[0, :, lanes]
        qm = [jnp.where(m, qp, jnp.zeros_like(qp)) for m in head_lanes]

        def step(j, carry, diagonal, lanes=lanes, qm=qm):
            acc, later = carry[0], list(carry[1:])
            kt = kt_ref[0, j, lanes, :]
            vp = v_ref[0, pl.ds(pl.multiple_of(j * blk, blk), blk), lanes]
            for hh in range(2):
                z = _dot(qm[hh], kt)
                sp = jnp.log1p(jnp.exp(-jnp.abs(z)))
                log_beta = jnp.minimum(z, 0.0) - sp
                log_keep = jnp.minimum(-z, 0.0) - sp
                if diagonal:
                    log_keep = jnp.where(lower, log_keep, 0.0)
                hi = log_keep.astype(BF16)
                lo = (log_keep - hi.astype(F32)).astype(BF16)
                between = _dot(hi, tri) + _dot(lo, tri) + later[hh]
                w = jnp.exp(log_beta + between)
                if diagonal:
                    w = jnp.where(lower, w, 0.0)
                vm = jnp.where(head_lanes[hh], vp, jnp.zeros_like(vp))
                acc = acc + _dot(w.astype(BF16), vm)
                later[hh] = later[hh] + jnp.sum(log_keep, axis=1, keepdims=True)
            return (acc, later[0], later[1])

        carry = (jnp.zeros((blk, LANES), F32), jnp.zeros((blk, 1), F32), jnp.zeros((blk, 1), F32))
        carry = step(i, carry, True)
        carry = lax.fori_loop(0, i, lambda n, c: step(i - 1 - n, c, False), carry)
        o_ref[0, :, lanes] = carry[0].astype(BF16)


def _attn(q, kt, v):
    b, s, _ = q.shape
    blk = ATT_BLOCK
    nb = s // blk
    return pl.pallas_call(
        _attn_kernel,
        grid=(b, nb),
        in_specs=[
            pl.BlockSpec((1, blk, SB_WIDTH), lambda bi, i: (bi, i, 0)),
            pl.BlockSpec((1, nb, SB_WIDTH, blk), lambda bi, i: (bi, 0, 0, 0)),
            pl.BlockSpec((1, s, SB_WIDTH), lambda bi, i: (bi, 0, 0)),
        ],
        out_specs=pl.BlockSpec((1, blk, SB_WIDTH), lambda bi, i: (bi, i, 0)),
        out_shape=jax.ShapeDtypeStruct((b, s, SB_WIDTH), BF16),
        compiler_params=pltpu.CompilerParams(
            dimension_semantics=("arbitrary", "arbitrary"), vmem_limit_bytes=VMEM_LIMIT),
        name="attn",
    )(q, kt, v)


def _mix_kernel(x_ref, bo_ref, pre_ref, post_ref, w_ref, bg_ref, pbd_ref, ps_ref, cw_ref, cb_ref,
                wbp_ref, wbs_ref, wbc_ref, wo_ref, o_ref, pool_halo, conv_halo):
    t = x_ref.shape[1]
    i = pl.program_id(1)

    @pl.when(i == 0)
    def _():
        pool_halo[...] = jnp.zeros_like(pool_halo)
        conv_halo[...] = jnp.zeros_like(conv_halo)

    x = x_ref[0]
    h = _rmsnorm(x, pre_ref[...]).astype(BF16)
    p = _dot(h, w_ref[...])

    u = p[:, :POOL_WIDTH]
    ext = jnp.concatenate([pool_halo[...], u], axis=0)
    lane = lax.broadcasted_iota(jnp.int32, (1, POOL_WIDTH), 1)
    win, width = ext, jnp.zeros((1, POOL_WIDTH), F32)
    acc = ext
    for g, wnd in enumerate(POOL_WINDOWS):
        acc = acc + pltpu.roll(acc, wnd // 2, 0)
        in_group = (lane >= g * POOL_GROUP_DIM) & (lane < (g + 1) * POOL_GROUP_DIM)
        win = jnp.where(in_group, acc, win)
        width = jnp.where(in_group, float(wnd), width)
    pos = (i * t + 1 + lax.broadcasted_iota(jnp.int32, (t, 1), 0)).astype(F32)
    pooled = win[POOL_HALO:] / jnp.minimum(pos, width) - u
    pool_halo[...] = u[t - POOL_HALO:]
    a_out = _dot(pooled.astype(BF16), pbd_ref[...]) * ps_ref[...]

    xc = p[:, POOL_WIDTH:POOL_WIDTH + CONV_WIDTH]
    gb = p[:, POOL_WIDTH + CONV_WIDTH:POOL_WIDTH + 2 * CONV_WIDTH]
    gc = p[:, POOL_WIDTH + 2 * CONV_WIDTH:POOL_WIDTH + 3 * CONV_WIDTH]
    uc = gc * xc
    extc = jnp.concatenate([conv_halo[...], uc], axis=0)
    y = (cw_ref[0:1, :] * pltpu.roll(extc, 2, 0) + cw_ref[1:2, :] * pltpu.roll(extc, 1, 0)
         + cw_ref[2:3, :] * extc)
    c_out = gb * (cb_ref[...] + y[CONV_HALO:])
    conv_halo[...] = uc[t - CONV_HALO:]

    off = POOL_WIDTH + 3 * CONV_WIDTH
    gates = _sigmoid(p[:, off:] + bg_ref[...])
    merged = (gates[:, :D_MODEL] * _dot(a_out.astype(BF16), wbp_ref[...])
              + gates[:, D_MODEL:2 * D_MODEL] * _dot(bo_ref[0], wbs_ref[...])
              + gates[:, 2 * D_MODEL:] * _dot(c_out.astype(BF16), wbc_ref[...]))
    out = _dot(merged.astype(BF16), wo_ref[...])
    o_ref[0] = x + _rmsnorm(out, post_ref[...])


def _mix(x, bo, pre_g, post_g, w_rest, b_gate, pool_bd, pool_scale, conv_w, conv_b,
         w_br_pool, w_br_sb, w_br_conv, w_out):
    b, s, d = x.shape
    t = min(SEQ_TILE, s)
    assert s % t == 0 and t >= POOL_HALO
    consts = (pre_g, post_g, w_rest, b_gate, pool_bd, pool_scale, conv_w, conv_b,
              w_br_pool, w_br_sb, w_br_conv, w_out)
    return pl.pallas_call(
        _mix_kernel,
        grid=(b, s // t),
        in_specs=[
            pl.BlockSpec((1, t, d), lambda bi, i: (bi, i, 0)),
            pl.BlockSpec((1, t, SB_WIDTH), lambda bi, i: (bi, i, 0)),
        ] + [_resident(c.shape) for c in consts],
        out_specs=pl.BlockSpec((1, t, d), lambda bi, i: (bi, i, 0)),
        out_shape=jax.ShapeDtypeStruct((b, s, d), F32),
        scratch_shapes=[pltpu.VMEM((POOL_HALO, POOL_WIDTH), F32),
                        pltpu.VMEM((CONV_HALO, CONV_WIDTH), F32)],
        compiler_params=pltpu.CompilerParams(
            dimension_semantics=("arbitrary", "arbitrary"), vmem_limit_bytes=VMEM_LIMIT),
        name="mix",
    )(x, bo, *consts)


def kernel(x, ffn1_pre_g, ffn1_post_g, ffn1_w_gate, ffn1_w_up, ffn1_w_down, mix_pre_g, mix_post_g,
           w_in, b_gate, pool_w, pool_scale, conv_w, conv_b, w_br_pool, w_br_sb, w_br_conv, w_out,
           ffn2_pre_g, ffn2_post_g, ffn2_w_gate, ffn2_w_up, ffn2_w_down):
    b, s, d = x.shape
    depth = w_in.shape[0]
    row = lambda a: a.reshape(1, -1)
    for l in range(depth):
        x = _ffn(x.reshape(b * s, d), row(ffn1_pre_g[l]), row(ffn1_post_g[l]),
                 ffn1_w_gate[l].astype(BF16), ffn1_w_up[l].astype(BF16),
                 ffn1_w_down[l].astype(BF16)).reshape(b, s, d)

        w_l = w_in[l].astype(BF16)
        w_rest = jnp.concatenate([w_l[:, :OFF_Q], w_l[:, OFF_CX:]], axis=1)
        pool_bd = jax.scipy.linalg.block_diag(*[pool_w[l, g] for g in range(len(POOL_WINDOWS))])
        q, kt, v = _qkv(x, row(mix_pre_g[l]), w_l[:, OFF_Q:OFF_CX])
        bo = _attn(q, kt, v)
        x = _mix(x, bo, row(mix_pre_g[l]), row(mix_post_g[l]), w_rest, row(b_gate[l]),
                 pool_bd.astype(BF16), row(pool_scale[l]), conv_w[l], row(conv_b[l]),
                 w_br_pool[l].astype(BF16), w_br_sb[l].astype(BF16), w_br_conv[l].astype(BF16),
                 w_out[l].astype(BF16))

        x = _ffn(x.reshape(b * s, d), row(ffn2_pre_g[l]), row(ffn2_post_g[l]),
                 ffn2_w_gate[l].astype(BF16), ffn2_w_up[l].astype(BF16),
                 ffn2_w_down[l].astype(BF16)).reshape(b, s, d)
    return x
```

```python
import functools

import jax
import jax.numpy as jnp
from jax import lax
from jax.experimental import pallas as pl
from jax.experimental.pallas import tpu as pltpu

F32 = jnp.float32
BF16 = jnp.bfloat16

EPS = 1e-6
D_MODEL = 1024
D_FF = 2816
FF_CHUNK = 256
POOL_WINDOWS = (2, 4, 8, 16)
POOL_GROUP_DIM = 64
POOL_WIDTH = 256
SB_HEADS = 8
SB_HEAD_DIM = 64
SB_WIDTH = 512
CONV_WIDTH = 256
N_BRANCH = 3
OFF_Q = POOL_WIDTH
OFF_CX = OFF_Q + 3 * SB_WIDTH
OFF_GATE = OFF_CX + 3 * CONV_WIDTH
LANES = 128
POOL_HALO = 16
CONV_HALO = 8

FFN_ROWS = 512
SEQ_TILE = 256
ATT_BLOCK = 256
VMEM_LIMIT = 56 * 1024 * 1024
LIVE_MIN = -105.0


def _rmsnorm(x, g):
    ms = jnp.mean(x * x, axis=-1, keepdims=True)
    return x * lax.rsqrt(ms + EPS) * g


def _sigmoid(x):
    return 1.0 / (1.0 + jnp.exp(-x))


def _dot(a, b):
    return jnp.dot(a, b, preferred_element_type=F32)


def _resident(shape):
    nd = len(shape)
    return pl.BlockSpec(shape, lambda *_: (0,) * nd, pipeline_mode=pl.Buffered(1))


def _ffn_kernel(x_ref, pre_ref, post_ref, wg_ref, wu_ref, wd_ref, o_ref, a_scr):
    x = x_ref[...]
    h = _rmsnorm(x, pre_ref[...]).astype(BF16)
    for c in range(D_FF // FF_CHUNK):
        cols = slice(c * FF_CHUNK, (c + 1) * FF_CHUNK)
        g = _dot(h, wg_ref[:, cols])
        u = _dot(h, wu_ref[:, cols])
        a_scr[:, cols] = ((g * _sigmoid(g)) * u).astype(BF16)
    y = _dot(a_scr[...], wd_ref[...])
    o_ref[...] = x + 0.5 * _rmsnorm(y, post_ref[...])


def _ffn(x2d, pre_g, post_g, wg, wu, wd):
    n, d = x2d.shape
    rows = min(FFN_ROWS, n)
    assert n % rows == 0
    return pl.pallas_call(
        _ffn_kernel,
        grid=(n // rows,),
        in_specs=[
            pl.BlockSpec((rows, d), lambda i: (i, 0)),
            _resident((1, d)), _resident((1, d)),
            _resident(wg.shape), _resident(wu.shape), _resident(wd.shape),
        ],
        out_specs=pl.BlockSpec((rows, d), lambda i: (i, 0)),
        out_shape=jax.ShapeDtypeStruct((n, d), F32),
        scratch_shapes=[pltpu.VMEM((rows, D_FF), BF16)],
        compiler_params=pltpu.CompilerParams(
            dimension_semantics=("arbitrary",), vmem_limit_bytes=VMEM_LIMIT),
        name="ffn",
    )(x2d, pre_g, post_g, wg, wu, wd)


def _qkv_kernel(x_ref, g_ref, w_ref, q_ref, kt_ref, v_ref):
    h = _rmsnorm(x_ref[0], g_ref[...]).astype(BF16)
    qkv = _dot(h, w_ref[...])
    q_ref[0] = (qkv[:, :SB_WIDTH] * (SB_HEAD_DIM ** -0.5)).astype(BF16)
    kt_ref[0, 0] = qkv[:, SB_WIDTH:2 * SB_WIDTH].T.astype(BF16)
    v_ref[0] = qkv[:, 2 * SB_WIDTH:].astype(BF16)


def _qkv(x, g, w):
    b, s, d = x.shape
    t = ATT_BLOCK
    assert s % t == 0
    return pl.pallas_call(
        _qkv_kernel,
        grid=(b, s // t),
        in_specs=[
            pl.BlockSpec((1, t, d), lambda bi, i: (bi, i, 0)),
            _resident((1, d)), _resident(w.shape),
        ],
        out_specs=[
            pl.BlockSpec((1, t, SB_WIDTH), lambda bi, i: (bi, i, 0)),
            pl.BlockSpec((1, 1, SB_WIDTH, t), lambda bi, i: (bi, i, 0, 0)),
            pl.BlockSpec((1, t, SB_WIDTH), lambda bi, i: (bi, i, 0)),
        ],
        out_shape=[
            jax.ShapeDtypeStruct((b, s, SB_WIDTH), BF16),
            jax.ShapeDtypeStruct((b, s // t, SB_WIDTH, t), BF16),
            jax.ShapeDtypeStruct((b, s, SB_WIDTH), BF16),
        ],
        compiler_params=pltpu.CompilerParams(
            dimension_semantics=("arbitrary", "arbitrary"), vmem_limit_bytes=VMEM_LIMIT),
        name="qkv",
    )(x, g, w)


def _attn_kernel(q_ref, kt_ref, v_ref, o_ref, acc_scr, later_scr):
    blk = ATT_BLOCK
    i = pl.program_id(1)
    row = lax.broadcasted_iota(jnp.int32, (blk, blk), 0)
    col = lax.broadcasted_iota(jnp.int32, (blk, blk), 1)
    lower = row > col
    ntri = jnp.where(lower, -1.0, 0.0).astype(BF16)
    ntri2 = jnp.concatenate([ntri, ntri], axis=0)
    lane = lax.broadcasted_iota(jnp.int32, (1, LANES), 1)
    head_lanes = (lane < SB_HEAD_DIM, lane >= SB_HEAD_DIM)
    sign_bit = jnp.uint32(0x80000000)

    def block(j, diagonal):
        rows = pl.ds(pl.multiple_of(j * blk, blk), blk)

        def scores(h):
            lanes = slice((h // 2) * LANES, (h // 2 + 1) * LANES)
            qp = q_ref[0, :, lanes]
            qm = jnp.where(head_lanes[h % 2], qp, jnp.zeros_like(qp))
            z = _dot(qm, kt_ref[0, j, lanes, :])
            neg_abs = lax.bitcast_convert_type(lax.bitcast_convert_type(z, jnp.uint32) | sign_bit, F32)
            p = jnp.maximum(z, 0.0) + jnp.log(1.0 + jnp.exp(neg_abs))
            if diagonal:
                p = jnp.where(lower, p, 0.0)
            hi = p.astype(BF16)
            lo = (p - hi.astype(F32)).astype(BF16)
            later = later_scr[h]
            later_scr[h] = later - jnp.sum(p, axis=1, keepdims=True)
            return jnp.concatenate([hi, lo], axis=1), (z - p) + later

        def weights(hl, lb):
            w = jnp.exp(lb + _dot(hl, ntri2))
            if diagonal:
                w = jnp.where(lower, w, 0.0)
            return w.astype(BF16)

        def values(h, w):
            lanes = slice((h // 2) * LANES, (h // 2 + 1) * LANES)
            vp = v_ref[0, rows, lanes]
            vm = jnp.where(head_lanes[h % 2], vp, jnp.zeros_like(vp))
            acc_scr[h // 2] += _dot(w, vm)

        stage1, stage2 = {}, {}
        for n in range(SB_HEADS + 2):
            if n < SB_HEADS:
                stage1[n] = scores(n)
            if 0 <= n - 1 < SB_HEADS:
                stage2[n - 1] = weights(*stage1.pop(n - 1))
            if 0 <= n - 2:
                values(n - 2, stage2.pop(n - 2))

    def live():
        return (jnp.max(later_scr[...]) > LIVE_MIN).astype(jnp.int32)

    acc_scr[...] = jnp.zeros_like(acc_scr)
    later_scr[...] = jnp.zeros_like(later_scr)
    block(i, True)

    def body(state):
        block(state[0], False)
        return state[0] - 1, live()

    lax.while_loop(lambda s: (s[0] >= 0) & (s[1] > 0), body, (i - 1, live()))
    for hp in range(SB_WIDTH // LANES):
        o_ref[0, :, hp * LANES:(hp + 1) * LANES] = acc_scr[hp].astype(BF16)


def _attn(q, kt, v):
    b, s, _ = q.shape
    blk = ATT_BLOCK
    nb = s // blk
    return pl.pallas_call(
        _attn_kernel,
        grid=(b, nb),
        in_specs=[
            pl.BlockSpec((1, blk, SB_WIDTH), lambda bi, i: (bi, i, 0)),
            pl.BlockSpec((1, nb, SB_WIDTH, blk), lambda bi, i: (bi, 0, 0, 0)),
            pl.BlockSpec((1, s, SB_WIDTH), lambda bi, i: (bi, 0, 0)),
        ],
        out_specs=pl.BlockSpec((1, blk, SB_WIDTH), lambda bi, i: (bi, i, 0)),
        out_shape=jax.ShapeDtypeStruct((b, s, SB_WIDTH), BF16),
        scratch_shapes=[pltpu.VMEM((SB_WIDTH // LANES, blk, LANES), F32),
                        pltpu.VMEM((SB_HEADS, blk, 1), F32)],
        compiler_params=pltpu.CompilerParams(
            dimension_semantics=("arbitrary", "arbitrary"), vmem_limit_bytes=VMEM_LIMIT),
        name="attn",
    )(q, kt, v)


def _mix_kernel(x_ref, bo_ref, pre_ref, post_ref, w_ref, bg_ref, pbd_ref, ps_ref, cw_ref, cb_ref,
                wbp_ref, wbs_ref, wbc_ref, wo_ref, o_ref, pool_halo, conv_halo):
    t = x_ref.shape[1]
    i = pl.program_id(1)

    @pl.when(i == 0)
    def _():
        pool_halo[...] = jnp.zeros_like(pool_halo)
        conv_halo[...] = jnp.zeros_like(conv_halo)

    x = x_ref[0]
    h = _rmsnorm(x, pre_ref[...]).astype(BF16)
    p = _dot(h, w_ref[...])

    u = p[:, :POOL_WIDTH]
    ext = jnp.concatenate([pool_halo[...], u], axis=0)
    lane = lax.broadcasted_iota(jnp.int32, (1, POOL_WIDTH), 1)
    win, width = ext, jnp.zeros((1, POOL_WIDTH), F32)
    acc = ext
    for g, wnd in enumerate(POOL_WINDOWS):
        acc = acc + pltpu.roll(acc, wnd // 2, 0)
        in_group = (lane >= g * POOL_GROUP_DIM) & (lane < (g + 1) * POOL_GROUP_DIM)
        win = jnp.where(in_group, acc, win)
        width = jnp.where(in_group, float(wnd), width)
    pos = (i * t + 1 + lax.broadcasted_iota(jnp.int32, (t, 1), 0)).astype(F32)
    pooled = win[POOL_HALO:] / jnp.minimum(pos, width) - u
    pool_halo[...] = u[t - POOL_HALO:]
    a_out = _dot(pooled.astype(BF16), pbd_ref[...]) * ps_ref[...]

    xc = p[:, POOL_WIDTH:POOL_WIDTH + CONV_WIDTH]
    gb = p[:, POOL_WIDTH + CONV_WIDTH:POOL_WIDTH + 2 * CONV_WIDTH]
    gc = p[:, POOL_WIDTH + 2 * CONV_WIDTH:POOL_WIDTH + 3 * CONV_WIDTH]
    uc = gc * xc
    extc = jnp.concatenate([conv_halo[...], uc], axis=0)
    y = (cw_ref[0:1, :] * pltpu.roll(extc, 2, 0) + cw_ref[1:2, :] * pltpu.roll(extc, 1, 0)
         + cw_ref[2:3, :] * extc)
    c_out = gb * (cb_ref[...] + y[CONV_HALO:])
    conv_halo[...] = uc[t - CONV_HALO:]

    off = POOL_WIDTH + 3 * CONV_WIDTH
    gates = _sigmoid(p[:, off:] + bg_ref[...])
    merged = (gates[:, :D_MODEL] * _dot(a_out.astype(BF16), wbp_ref[...])
              + gates[:, D_MODEL:2 * D_MODEL] * _dot(bo_ref[0], wbs_ref[...])
              + gates[:, 2 * D_MODEL:] * _dot(c_out.astype(BF16), wbc_ref[...]))
    out = _dot(merged.astype(BF16), wo_ref[...])
    o_ref[0] = x + _rmsnorm(out, post_ref[...])


def _mix(x, bo, pre_g, post_g, w_rest, b_gate, pool_bd, pool_scale, conv_w, conv_b,
         w_br_pool, w_br_sb, w_br_conv, w_out):
    b, s, d = x.shape
    t = min(SEQ_TILE, s)
    assert s % t == 0 and t >= POOL_HALO
    consts = (pre_g, post_g, w_rest, b_gate, pool_bd, pool_scale, conv_w, conv_b,
              w_br_pool, w_br_sb, w_br_conv, w_out)
    return pl.pallas_call(
        _mix_kernel,
        grid=(b, s // t),
        in_specs=[
            pl.BlockSpec((1, t, d), lambda bi, i: (bi, i, 0)),
            pl.BlockSpec((1, t, SB_WIDTH), lambda bi, i: (bi, i, 0)),
        ] + [_resident(c.shape) for c in consts],
        out_specs=pl.BlockSpec((1, t, d), lambda bi, i: (bi, i, 0)),
        out_shape=jax.ShapeDtypeStruct((b, s, d), F32),
        scratch_shapes=[pltpu.VMEM((POOL_HALO, POOL_WIDTH), F32),
                        pltpu.VMEM((CONV_HALO, CONV_WIDTH), F32)],
        compiler_params=pltpu.CompilerParams(
            dimension_semantics=("arbitrary", "arbitrary"), vmem_limit_bytes=VMEM_LIMIT),
        name="mix",
    )(x, bo, *consts)


def kernel(x, ffn1_pre_g, ffn1_post_g, ffn1_w_gate, ffn1_w_up, ffn1_w_down, mix_pre_g, mix_post_g,
           w_in, b_gate, pool_w, pool_scale, conv_w, conv_b, w_br_pool, w_br_sb, w_br_conv, w_out,
           ffn2_pre_g, ffn2_post_g, ffn2_w_gate, ffn2_w_up, ffn2_w_down):
    b, s, d = x.shape
    depth = w_in.shape[0]
    row = lambda a: a.reshape(1, -1)
    for l in range(depth):
        x = _ffn(x.reshape(b * s, d), row(ffn1_pre_g[l]), row(ffn1_post_g[l]),
                 ffn1_w_gate[l].astype(BF16), ffn1_w_up[l].astype(BF16),
                 ffn1_w_down[l].astype(BF16)).reshape(b, s, d)

        w_l = w_in[l].astype(BF16)
        w_rest = jnp.concatenate([w_l[:, :OFF_Q], w_l[:, OFF_CX:]], axis=1)
        pool_bd = jax.scipy.linalg.block_diag(*[pool_w[l, g] for g in range(len(POOL_WINDOWS))])
        q, kt, v = _qkv(x, row(mix_pre_g[l]), w_l[:, OFF_Q:OFF_CX])
        bo = _attn(q, kt, v)
        x = _mix(x, bo, row(mix_pre_g[l]), row(mix_post_g[l]), w_rest, row(b_gate[l]),
                 pool_bd.astype(BF16), row(pool_scale[l]), conv_w[l], row(conv_b[l]),
                 w_br_pool[l].astype(BF16), w_br_sb[l].astype(BF16), w_br_conv[l].astype(BF16),
                 w_out[l].astype(BF16))

        x = _ffn(x.reshape(b * s, d), row(ffn2_pre_g[l]), row(ffn2_post_g[l]),
                 ffn2_w_gate[l].astype(BF16), ffn2_w_up[l].astype(BF16),
                 ffn2_w_down[l].astype(BF16)).reshape(b, s, d)
    return x
```

```python
import functools

import jax
import jax.numpy as jnp
from jax import lax
from jax.experimental import pallas as pl
from jax.experimental.pallas import tpu as pltpu

F32 = jnp.float32
BF16 = jnp.bfloat16

EPS = 1e-6
D_MODEL = 1024
D_FF = 2816
FF_CHUNK = 256
POOL_WINDOWS = (2, 4, 8, 16)
POOL_GROUP_DIM = 64
POOL_WIDTH = 256
SB_HEADS = 8
SB_HEAD_DIM = 64
SB_WIDTH = 512
CONV_WIDTH = 256
N_BRANCH = 3
OFF_Q = POOL_WIDTH
OFF_CX = OFF_Q + 3 * SB_WIDTH
OFF_GATE = OFF_CX + 3 * CONV_WIDTH
LANES = 128
POOL_HALO = 16
CONV_HALO = 8

FFN_ROWS = 1024
QKV_ROWS = 1024
SEQ_TILE = 512
ATT_BLOCK = 256
VMEM_LIMIT = 56 * 1024 * 1024
LIVE_MIN = -105.0


def _rmsnorm(x, g):
    ms = jnp.mean(x * x, axis=-1, keepdims=True)
    return x * lax.rsqrt(ms + EPS) * g


def _sigmoid(x):
    return 1.0 / (1.0 + jnp.exp(-x))


def _dot(a, b):
    return jnp.dot(a, b, preferred_element_type=F32)


def _layer_block(stacked, layer):
    tail = stacked.shape[1:]
    return pl.BlockSpec((None,) + tail, lambda *_: (layer,) + (0,) * len(tail),
                        pipeline_mode=pl.Buffered(1))


def _ffn_kernel(x_ref, pre_ref, post_ref, wg_ref, wu_ref, wd_ref, o_ref, a_scr):
    half = x_ref.shape[0] // 2
    parts = [slice(0, half), slice(half, 2 * half)]
    xs = [x_ref[r, :] for r in parts]
    hs = [_rmsnorm(x, pre_ref[...]).astype(BF16) for x in xs]
    for r, h in zip(parts, hs):
        for c in range(D_FF // FF_CHUNK):
            cols = slice(c * FF_CHUNK, (c + 1) * FF_CHUNK)
            g = _dot(h, wg_ref[:, cols])
            u = _dot(h, wu_ref[:, cols])
            a_scr[r, cols] = ((g * _sigmoid(g)) * u).astype(BF16)
    ys = [_dot(a_scr[r, :], wd_ref[...]) for r in parts]
    for r, x, y in zip(parts, xs, ys):
        o_ref[r, :] = x + 0.5 * _rmsnorm(y, post_ref[...])


def _ffn(x2d, layer, pre_g, post_g, wg, wu, wd):
    n, d = x2d.shape
    rows = min(FFN_ROWS, n)
    assert n % rows == 0
    return pl.pallas_call(
        _ffn_kernel,
        grid=(n // rows,),
        in_specs=[pl.BlockSpec((rows, d), lambda i: (i, 0))]
        + [_layer_block(a, layer) for a in (pre_g, post_g, wg, wu, wd)],
        out_specs=pl.BlockSpec((rows, d), lambda i: (i, 0)),
        out_shape=jax.ShapeDtypeStruct((n, d), F32),
        scratch_shapes=[pltpu.VMEM((rows, D_FF), BF16)],
        compiler_params=pltpu.CompilerParams(
            dimension_semantics=("arbitrary",), vmem_limit_bytes=VMEM_LIMIT),
        name="ffn",
    )(x2d, pre_g, post_g, wg, wu, wd)


def _qkv_kernel(x_ref, g_ref, w_ref, q_ref, kt_ref, v_ref):
    h = _rmsnorm(x_ref[...], g_ref[...]).astype(BF16)
    qkv = _dot(h, w_ref[:, OFF_Q:OFF_CX])
    q_ref[...] = (qkv[:, :SB_WIDTH] * (SB_HEAD_DIM ** -0.5)).astype(BF16)
    for c in range(kt_ref.shape[0]):
        k = qkv[c * ATT_BLOCK:(c + 1) * ATT_BLOCK, SB_WIDTH:2 * SB_WIDTH]
        kt_ref[c] = k.T.astype(BF16)
    v_ref[...] = qkv[:, 2 * SB_WIDTH:].astype(BF16)


def _qkv(x2d, layer, g, w_in):
    n, d = x2d.shape
    rows = min(QKV_ROWS, n)
    assert n % rows == 0 and rows % ATT_BLOCK == 0
    nblk = rows // ATT_BLOCK
    return pl.pallas_call(
        _qkv_kernel,
        grid=(n // rows,),
        in_specs=[pl.BlockSpec((rows, d), lambda i: (i, 0)),
                  _layer_block(g, layer), _layer_block(w_in, layer)],
        out_specs=[
            pl.BlockSpec((rows, SB_WIDTH), lambda i: (i, 0)),
            pl.BlockSpec((nblk, SB_WIDTH, ATT_BLOCK), lambda i: (i, 0, 0)),
            pl.BlockSpec((rows, SB_WIDTH), lambda i: (i, 0)),
        ],
        out_shape=[
            jax.ShapeDtypeStruct((n, SB_WIDTH), BF16),
            jax.ShapeDtypeStruct((n // ATT_BLOCK, SB_WIDTH, ATT_BLOCK), BF16),
            jax.ShapeDtypeStruct((n, SB_WIDTH), BF16),
        ],
        compiler_params=pltpu.CompilerParams(
            dimension_semantics=("arbitrary",), vmem_limit_bytes=VMEM_LIMIT),
        name="qkv",
    )(x2d, g, w_in)


def _attn_kernel(q_ref, kt_ref, v_ref, o_ref, acc_scr, later_scr):
    blk = ATT_BLOCK
    i = pl.program_id(1)
    row = lax.broadcasted_iota(jnp.int32, (blk, blk), 0)
    col = lax.broadcasted_iota(jnp.int32, (blk, blk), 1)
    lower = row > col
    ntri = jnp.where(lower, -1.0, 0.0).astype(BF16)
    ntri2 = jnp.concatenate([ntri, ntri], axis=0)
    lane = lax.broadcasted_iota(jnp.int32, (1, LANES), 1)
    head_lanes = (lane < SB_HEAD_DIM, lane >= SB_HEAD_DIM)
    sign_bit = jnp.uint32(0x80000000)

    def block(j, diagonal):
        rows = pl.ds(pl.multiple_of(j * blk, blk), blk)

        def scores(h):
            lanes = slice((h // 2) * LANES, (h // 2 + 1) * LANES)
            qp = q_ref[0, :, lanes]
            qm = jnp.where(head_lanes[h % 2], qp, jnp.zeros_like(qp))
            z = _dot(qm, kt_ref[0, j, lanes, :])
            neg_abs = lax.bitcast_convert_type(lax.bitcast_convert_type(z, jnp.uint32) | sign_bit, F32)
            p = jnp.maximum(z, 0.0) + jnp.log(1.0 + jnp.exp(neg_abs))
            if diagonal:
                p = jnp.where(lower, p, 0.0)
            hi = p.astype(BF16)
            lo = (p - hi.astype(F32)).astype(BF16)
            later = later_scr[h]
            later_scr[h] = later - jnp.sum(p, axis=1, keepdims=True)
            return jnp.concatenate([hi, lo], axis=1), (z - p) + later

        def weights(hl, lb):
            w = jnp.exp(lb + _dot(hl, ntri2))
            if diagonal:
                w = jnp.where(lower, w, 0.0)
            return w.astype(BF16)

        def values(h, w):
            lanes = slice((h // 2) * LANES, (h // 2 + 1) * LANES)
            vp = v_ref[0, rows, lanes]
            vm = jnp.where(head_lanes[h % 2], vp, jnp.zeros_like(vp))
            acc_scr[h // 2] += _dot(w, vm)

        stage1, stage2 = {}, {}
        for n in range(SB_HEADS + 2):
            if n < SB_HEADS:
                stage1[n] = scores(n)
            if 0 <= n - 1 < SB_HEADS:
                stage2[n - 1] = weights(*stage1.pop(n - 1))
            if 0 <= n - 2:
                values(n - 2, stage2.pop(n - 2))

    def live():
        return (jnp.max(later_scr[...]) > LIVE_MIN).astype(jnp.int32)

    acc_scr[...] = jnp.zeros_like(acc_scr)
    later_scr[...] = jnp.zeros_like(later_scr)
    block(i, True)

    def body(state):
        block(state[0], False)
        return state[0] - 1, live()

    lax.while_loop(lambda s: (s[0] >= 0) & (s[1] > 0), body, (i - 1, live()))
    for hp in range(SB_WIDTH // LANES):
        o_ref[0, :, hp * LANES:(hp + 1) * LANES] = acc_scr[hp].astype(BF16)


def _attn(q, kt, v, batch):
    n = q.shape[0]
    s = n // batch
    blk = ATT_BLOCK
    nb = s // blk
    return pl.pallas_call(
        _attn_kernel,
        grid=(batch, nb),
        in_specs=[
            pl.BlockSpec((1, blk, SB_WIDTH), lambda bi, i: (bi, i, 0)),
            pl.BlockSpec((1, nb, SB_WIDTH, blk), lambda bi, i: (bi, 0, 0, 0)),
            pl.BlockSpec((1, s, SB_WIDTH), lambda bi, i: (bi, 0, 0)),
        ],
        out_specs=pl.BlockSpec((1, blk, SB_WIDTH), lambda bi, i: (bi, i, 0)),
        out_shape=jax.ShapeDtypeStruct((batch, s, SB_WIDTH), BF16),
        scratch_shapes=[pltpu.VMEM((SB_WIDTH // LANES, blk, LANES), F32),
                        pltpu.VMEM((SB_HEADS, blk, 1), F32)],
        compiler_params=pltpu.CompilerParams(
            dimension_semantics=("arbitrary", "arbitrary"), vmem_limit_bytes=VMEM_LIMIT),
        name="attn",
    )(q.reshape(batch, s, SB_WIDTH), kt.reshape(batch, nb, SB_WIDTH, blk),
      v.reshape(batch, s, SB_WIDTH))


def _mix_kernel(x_ref, bo_ref, pre_ref, post_ref, w_ref, bg_ref, pbd_ref, ps_ref, cw_ref, cb_ref,
                wbp_ref, wbs_ref, wbc_ref, wo_ref, o_ref, pool_halo, conv_halo):
    t = x_ref.shape[1]
    i = pl.program_id(1)

    @pl.when(i == 0)
    def _():
        pool_halo[...] = jnp.zeros_like(pool_halo)
        conv_halo[...] = jnp.zeros_like(conv_halo)

    x = x_ref[0]
    h = _rmsnorm(x, pre_ref[...]).astype(BF16)

    u = _dot(h, w_ref[:, :OFF_Q])
    p = _dot(h, w_ref[:, OFF_CX:])
    b_proj = _dot(bo_ref[0], wbs_ref[...])

    ext = jnp.concatenate([pool_halo[...], u], axis=0)
    lane = lax.broadcasted_iota(jnp.int32, (1, POOL_WIDTH), 1)
    win, width = ext, jnp.zeros((1, POOL_WIDTH), F32)
    acc = ext
    for g, wnd in enumerate(POOL_WINDOWS):
        acc = acc + pltpu.roll(acc, wnd // 2, 0)
        in_group = (lane >= g * POOL_GROUP_DIM) & (lane < (g + 1) * POOL_GROUP_DIM)
        win = jnp.where(in_group, acc, win)
        width = jnp.where(in_group, float(wnd), width)
    pos = (i * t + 1 + lax.broadcasted_iota(jnp.int32, (t, 1), 0)).astype(F32)
    pooled = win[POOL_HALO:] / jnp.minimum(pos, width) - u
    pool_halo[...] = u[t - POOL_HALO:]
    a_out = _dot(pooled.astype(BF16), pbd_ref[...]) * ps_ref[...]

    xc = p[:, :CONV_WIDTH]
    gb = p[:, CONV_WIDTH:2 * CONV_WIDTH]
    gc = p[:, 2 * CONV_WIDTH:3 * CONV_WIDTH]
    uc = gc * xc
    extc = jnp.concatenate([conv_halo[...], uc], axis=0)
    y = (cw_ref[0:1, :] * pltpu.roll(extc, 2, 0) + cw_ref[1:2, :] * pltpu.roll(extc, 1, 0)
         + cw_ref[2:3, :] * extc)
    c_out = gb * (cb_ref[...] + y[CONV_HALO:])
    conv_halo[...] = uc[t - CONV_HALO:]

    gates = _sigmoid(p[:, 3 * CONV_WIDTH:] + bg_ref[...])
    merged = (gates[:, :D_MODEL] * _dot(a_out.astype(BF16), wbp_ref[...])
              + gates[:, D_MODEL:2 * D_MODEL] * b_proj
              + gates[:, 2 * D_MODEL:] * _dot(c_out.astype(BF16), wbc_ref[...]))
    out = _dot(merged.astype(BF16), wo_ref[...])
    o_ref[0] = x + _rmsnorm(out, post_ref[...])


def _mix(x, bo, layer, *stacked):
    b, s, d = x.shape
    t = min(SEQ_TILE, s)
    assert s % t == 0 and t >= POOL_HALO
    return pl.pallas_call(
        _mix_kernel,
        grid=(b, s // t),
        in_specs=[
            pl.BlockSpec((1, t, d), lambda bi, i: (bi, i, 0)),
            pl.BlockSpec((1, t, SB_WIDTH), lambda bi, i: (bi, i, 0)),
        ] + [_layer_block(a, layer) for a in stacked],
        out_specs=pl.BlockSpec((1, t, d), lambda bi, i: (bi, i, 0)),
        out_shape=jax.ShapeDtypeStruct((b, s, d), F32),
        scratch_shapes=[pltpu.VMEM((POOL_HALO, POOL_WIDTH), F32),
                        pltpu.VMEM((CONV_HALO, CONV_WIDTH), F32)],
        compiler_params=pltpu.CompilerParams(
            dimension_semantics=("arbitrary", "arbitrary"), vmem_limit_bytes=VMEM_LIMIT),
        name="mix",
    )(x, bo, *stacked)


def kernel(x, ffn1_pre_g, ffn1_post_g, ffn1_w_gate, ffn1_w_up, ffn1_w_down, mix_pre_g, mix_post_g,
           w_in, b_gate, pool_w, pool_scale, conv_w, conv_b, w_br_pool, w_br_sb, w_br_conv, w_out,
           ffn2_pre_g, ffn2_post_g, ffn2_w_gate, ffn2_w_up, ffn2_w_down):
    b, s, d = x.shape
    depth = w_in.shape[0]
    rows = lambda a: a.reshape(depth, 1, -1)
    bf = lambda a: a.astype(BF16)
    groups = len(POOL_WINDOWS)
    pool_bd = jnp.einsum("lgcd,gh->lgchd", pool_w, jnp.eye(groups, dtype=pool_w.dtype))
    pool_bd = pool_bd.reshape(depth, POOL_WIDTH, POOL_WIDTH)
    ffn1 = (rows(ffn1_pre_g), rows(ffn1_post_g), bf(ffn1_w_gate), bf(ffn1_w_up), bf(ffn1_w_down))
    ffn2 = (rows(ffn2_pre_g), rows(ffn2_post_g), bf(ffn2_w_gate), bf(ffn2_w_up), bf(ffn2_w_down))
    w_in_bf = bf(w_in)
    mix = (rows(mix_pre_g), rows(mix_post_g), w_in_bf, rows(b_gate), bf(pool_bd), rows(pool_scale),
           conv_w, rows(conv_b), bf(w_br_pool), bf(w_br_sb), bf(w_br_conv), bf(w_out))
    x = x.reshape(b * s, d)
    for l in range(depth):
        x = _ffn(x, l, *ffn1)
        q, kt, v = _qkv(x, l, mix[0], w_in_bf)
        bo = _attn(q, kt, v, b)
        x = _mix(x.reshape(b, s, d), bo, l, *mix).reshape(b * s, d)
        x = _ffn(x, l, *ffn2)
    return x.reshape(b, s, d)
```

```python
import functools

import jax
import jax.numpy as jnp
from jax import lax
from jax.experimental import pallas as pl
from jax.experimental.pallas import tpu as pltpu

F32 = jnp.float32
BF16 = jnp.bfloat16

EPS = 1e-6
D_MODEL = 1024
D_FF = 2816
FF_CHUNK = 256
POOL_WINDOWS = (2, 4, 8, 16)
POOL_GROUP_DIM = 64
POOL_WIDTH = 256
SB_HEADS = 8
SB_HEAD_DIM = 64
SB_WIDTH = 512
CONV_WIDTH = 256
N_BRANCH = 3
OFF_Q = POOL_WIDTH
OFF_CX = OFF_Q + 3 * SB_WIDTH
OFF_GATE = OFF_CX + 3 * CONV_WIDTH
LANES = 128
POOL_HALO = 16
CONV_HALO = 8

FFN_ROWS = 1024
QKV_ROWS = 1024
SEQ_TILE = 512
ATT_BLOCK = 256
STAGE_SKEW = 1
P_CHUNK = 256
VMEM_LIMIT = 56 * 1024 * 1024
LIVE_MIN = -105.0
DEAD = -1e30


def _rmsnorm(x, g):
    ms = jnp.mean(x * x, axis=-1, keepdims=True)
    return x * lax.rsqrt(ms + EPS) * g


def _sigmoid(x):
    return 1.0 / (1.0 + jnp.exp(-x))


def _dot(a, b):
    return jnp.dot(a, b, preferred_element_type=F32)


def _layer_block(stacked, layer):
    tail = stacked.shape[1:]
    return pl.BlockSpec((None,) + tail, lambda *_: (layer,) + (0,) * len(tail),
                        pipeline_mode=pl.Buffered(1))


def _ffn_kernel(x_ref, pre_ref, post_ref, wg_ref, wu_ref, wd_ref, o_ref, a_scr):
    half = x_ref.shape[0] // 2
    parts = [slice(0, half), slice(half, 2 * half)]
    xs = [x_ref[r, :] for r in parts]
    hs = [_rmsnorm(x, pre_ref[...]).astype(BF16) for x in xs]
    for r, h in zip(parts, hs):
        for c in range(D_FF // FF_CHUNK):
            cols = slice(c * FF_CHUNK, (c + 1) * FF_CHUNK)
            g = _dot(h, wg_ref[:, cols])
            u = _dot(h, wu_ref[:, cols])
            a_scr[r, cols] = ((g * _sigmoid(g)) * u).astype(BF16)
    ys = [_dot(a_scr[r, :], wd_ref[...]) for r in parts]
    for r, x, y in zip(parts, xs, ys):
        o_ref[r, :] = x + 0.5 * _rmsnorm(y, post_ref[...])


def _ffn(x2d, layer, pre_g, post_g, wg, wu, wd):
    n, d = x2d.shape
    rows = min(FFN_ROWS, n)
    assert n % rows == 0
    return pl.pallas_call(
        _ffn_kernel,
        grid=(n // rows,),
        in_specs=[pl.BlockSpec((rows, d), lambda i: (i, 0))]
        + [_layer_block(a, layer) for a in (pre_g, post_g, wg, wu, wd)],
        out_specs=pl.BlockSpec((rows, d), lambda i: (i, 0)),
        out_shape=jax.ShapeDtypeStruct((n, d), F32),
        scratch_shapes=[pltpu.VMEM((rows, D_FF), BF16)],
        compiler_params=pltpu.CompilerParams(
            dimension_semantics=("arbitrary",), vmem_limit_bytes=VMEM_LIMIT),
        name="ffn",
    )(x2d, pre_g, post_g, wg, wu, wd)


def _qkv_kernel(x_ref, g_ref, w_ref, q_ref, kt_ref, v_ref):
    h = _rmsnorm(x_ref[...], g_ref[...]).astype(BF16)
    qkv = _dot(h, w_ref[:, OFF_Q:OFF_CX])
    q_ref[...] = (qkv[:, :SB_WIDTH] * (SB_HEAD_DIM ** -0.5)).astype(BF16)
    for c in range(kt_ref.shape[0]):
        k = qkv[c * ATT_BLOCK:(c + 1) * ATT_BLOCK, SB_WIDTH:2 * SB_WIDTH]
        kt_ref[c] = k.T.astype(BF16)
    v_ref[...] = qkv[:, 2 * SB_WIDTH:].astype(BF16)


def _qkv(x2d, layer, g, w_in):
    n, d = x2d.shape
    rows = min(QKV_ROWS, n)
    assert n % rows == 0 and rows % ATT_BLOCK == 0
    nblk = rows // ATT_BLOCK
    return pl.pallas_call(
        _qkv_kernel,
        grid=(n // rows,),
        in_specs=[pl.BlockSpec((rows, d), lambda i: (i, 0)),
                  _layer_block(g, layer), _layer_block(w_in, layer)],
        out_specs=[
            pl.BlockSpec((rows, SB_WIDTH), lambda i: (i, 0)),
            pl.BlockSpec((nblk, SB_WIDTH, ATT_BLOCK), lambda i: (i, 0, 0)),
            pl.BlockSpec((rows, SB_WIDTH), lambda i: (i, 0)),
        ],
        out_shape=[
            jax.ShapeDtypeStruct((n, SB_WIDTH), BF16),
            jax.ShapeDtypeStruct((n // ATT_BLOCK, SB_WIDTH, ATT_BLOCK), BF16),
            jax.ShapeDtypeStruct((n, SB_WIDTH), BF16),
        ],
        compiler_params=pltpu.CompilerParams(
            dimension_semantics=("arbitrary",), vmem_limit_bytes=VMEM_LIMIT),
        name="qkv",
    )(x2d, g, w_in)


class _Sweep:
    def __init__(self, q_ref, kt_ref, v_ref, acc_scr, later_scr):
        self.q_ref, self.kt_ref, self.v_ref = q_ref, kt_ref, v_ref
        self.acc_scr, self.later_scr = acc_scr, later_scr
        blk = ATT_BLOCK
        row = lax.broadcasted_iota(jnp.int32, (blk, blk), 0)
        col = lax.broadcasted_iota(jnp.int32, (blk, blk), 1)
        lower = row > col
        ntri = jnp.where(lower, -1.0, 0.0).astype(BF16)
        self.ntri2 = jnp.concatenate([ntri, ntri], axis=0)
        self.lower_half = lower[:blk // 2, :blk // 2]
        lane = lax.broadcasted_iota(jnp.int32, (1, LANES), 1)
        self.head_lanes = (lane < SB_HEAD_DIM, lane >= SB_HEAD_DIM)

    @staticmethod
    def _split(x, diagonal):
        half = ATT_BLOCK // 2
        return (x[:half, :half], x[half:, :half], x[half:, half:]) if diagonal else (x,)

    @staticmethod
    def _join(pieces, diagonal):
        if not diagonal:
            return pieces[0]
        tl, bl, br = pieces
        top = jnp.concatenate([tl, jnp.zeros_like(tl)], axis=1)
        return jnp.concatenate([top, jnp.concatenate([bl, br], axis=1)], axis=0)

    def _masks(self, diagonal):
        return (self.lower_half, None, self.lower_half) if diagonal else (None,)

    def scores(self, r, h, j, diagonal, guard):
        blk = ATT_BLOCK
        lanes = slice((h // 2) * LANES, (h // 2 + 1) * LANES)
        qp = self.q_ref[0, r * blk:(r + 1) * blk, lanes]
        qm = jnp.where(self.head_lanes[h % 2], qp, jnp.zeros_like(qp))
        z = _dot(qm, self.kt_ref[0, j, lanes, :])
        later = None if diagonal else self.later_scr[r * SB_HEADS + h]
        if guard is not None:
            later = jnp.where(guard, later, DEAD)
        ps, his, los, lbs = [], [], [], []
        for zq, mask in zip(self._split(z, diagonal), self._masks(diagonal)):
            neg_abs = lax.bitcast_convert_type(
                lax.bitcast_convert_type(zq, jnp.uint32) | jnp.uint32(0x80000000), F32)
            p = jnp.maximum(zq, 0.0) + jnp.log(1.0 + jnp.exp(neg_abs))
            if mask is not None:
                p = jnp.where(mask, p, 0.0)
            hi = p.astype(BF16)
            ps.append(p)
            his.append(hi)
            los.append((p - hi.astype(F32)).astype(BF16))
            lbs.append(zq - p if diagonal else (zq - p) + later)
        spent = jnp.sum(self._join(ps, diagonal), axis=1, keepdims=True)
        later = -spent if diagonal else later - spent
        self.later_scr[r * SB_HEADS + h] = later
        hl = jnp.concatenate([self._join(his, diagonal), self._join(los, diagonal)], axis=1)
        return (hl, lbs), later

    def weights(self, hl, lbs, diagonal):
        ws = []
        for lb, between, mask in zip(lbs, self._split(_dot(hl, self.ntri2), diagonal),
                                     self._masks(diagonal)):
            w = jnp.exp(lb + between)
            if mask is not None:
                w = jnp.where(mask, w, 0.0)
            ws.append(w.astype(BF16))
        return self._join(ws, diagonal)

    def values(self, r, h, j, diagonal, w):
        blk = ATT_BLOCK
        lanes = slice((h // 2) * LANES, (h // 2 + 1) * LANES)
        vp = self.v_ref[0, pl.ds(pl.multiple_of(j * blk, blk), blk), lanes]
        vm = jnp.where(self.head_lanes[h % 2], vp, jnp.zeros_like(vp))
        rows = slice(r * blk, (r + 1) * blk)
        if diagonal and h % 2 == 0:
            self.acc_scr[h // 2, rows, :] = _dot(w, vm)
        else:
            self.acc_scr[h // 2, rows, :] += _dot(w, vm)

    def run(self, groups, fillers=()):
        units = [(r, h, j, diagonal, guard) for r, j, diagonal, guard in groups
                 for h in range(SB_HEADS)]
        fillers, filled = list(fillers), []
        stage1, stage2, latest = {}, {}, {}
        for n in range(len(units) + 2 * STAGE_SKEW):
            if n < len(units):
                r, h, j, diagonal, guard = units[n]
                stage1[n], later = self.scores(r, h, j, diagonal, guard)
                latest[r] = later if h == 0 else jnp.maximum(latest[r], later)
            while len(filled) < min(len(fillers), (n + 1) * len(fillers) // len(units)):
                filled.append(fillers[len(filled)]())
            m = n - STAGE_SKEW
            if 0 <= m < len(units):
                stage2[m] = self.weights(*stage1.pop(m), units[m][3])
            m = n - 2 * STAGE_SKEW
            if 0 <= m:
                r, h, j, diagonal, _ = units[m]
                self.values(r, h, j, diagonal, stage2.pop(m))
        live = {r: (jnp.max(v) > LIVE_MIN).astype(jnp.int32) for r, v in latest.items()}
        return live, filled


def _mix_kernel(x_ref, q_ref, kt_ref, v_ref, pre_ref, post_ref, w_ref, bg_ref, pbd_ref, ps_ref,
                cw_ref, cb_ref, wbp_ref, wbs_ref, wbc_ref, wo_ref, o_ref,
                pool_halo, conv_halo, acc_scr, later_scr):
    t = x_ref.shape[1]
    i = pl.program_id(1)
    nq = t // ATT_BLOCK

    @pl.when(i == 0)
    def _():
        pool_halo[...] = jnp.zeros_like(pool_halo)
        conv_halo[...] = jnp.zeros_like(conv_halo)

    lane = lax.broadcasted_iota(jnp.int32, (1, POOL_WIDTH), 1)
    x = x_ref[0]
    h = _rmsnorm(x, pre_ref[...]).astype(BF16)
    u = _dot(h, w_ref[:, :OFF_Q])

    sweep = _Sweep(q_ref, kt_ref, v_ref, acc_scr, later_scr)
    groups = []
    for r in range(nq):
        qb = nq * i + r
        guard = (i > 0) if r == 0 else None
        groups += [(r, qb, True, None), (r, jnp.maximum(qb - 1, 0), False, guard)]
    n_chunks = (w_ref.shape[1] - OFF_CX) // P_CHUNK
    chunks = [functools.partial(
        lambda c: _dot(h, w_ref[:, OFF_CX + c * P_CHUNK:OFF_CX + (c + 1) * P_CHUNK]), c)
        for c in range(n_chunks)]
    live, p_chunks = sweep.run(groups, chunks)
    p = jnp.concatenate(p_chunks, axis=1)

    ext = jnp.concatenate([pool_halo[...], u], axis=0)
    win, width = ext, jnp.zeros((1, POOL_WIDTH), F32)
    acc = ext
    for g, wnd in enumerate(POOL_WINDOWS):
        acc = acc + pltpu.roll(acc, wnd // 2, 0)
        in_group = (lane >= g * POOL_GROUP_DIM) & (lane < (g + 1) * POOL_GROUP_DIM)
        win = jnp.where(in_group, acc, win)
        width = jnp.where(in_group, float(wnd), width)
    pos = (i * t + 1 + lax.broadcasted_iota(jnp.int32, (t, 1), 0)).astype(F32)
    pooled = win[POOL_HALO:] / jnp.minimum(pos, width) - u
    pool_halo[...] = u[t - POOL_HALO:]
    a_out = _dot(pooled.astype(BF16), pbd_ref[...]) * ps_ref[...]

    xc = p[:, :CONV_WIDTH]
    gb = p[:, CONV_WIDTH:2 * CONV_WIDTH]
    gc = p[:, 2 * CONV_WIDTH:3 * CONV_WIDTH]
    uc = gc * xc
    extc = jnp.concatenate([conv_halo[...], uc], axis=0)
    y = (cw_ref[0:1, :] * pltpu.roll(extc, 2, 0) + cw_ref[1:2, :] * pltpu.roll(extc, 1, 0)
         + cw_ref[2:3, :] * extc)
    c_out = gb * (cb_ref[...] + y[CONV_HALO:])
    conv_halo[...] = uc[t - CONV_HALO:]

    gates = _sigmoid(p[:, 3 * CONV_WIDTH:] + bg_ref[...])
    a_gated = gates[:, :D_MODEL] * _dot(a_out.astype(BF16), wbp_ref[...])
    c_gated = gates[:, 2 * D_MODEL:] * _dot(c_out.astype(BF16), wbc_ref[...])
    b_gate = gates[:, D_MODEL:2 * D_MODEL]

    for r in range(nq):
        def body(s, r=r):
            now, _ = sweep.run([(r, s[0], False, None)])
            return s[0] - 1, now[r]
        lax.while_loop(lambda s: (s[0] >= 0) & (s[1] > 0), body, (nq * i + r - 2, live[r]))

    bo = jnp.concatenate([acc_scr[hp] for hp in range(SB_WIDTH // LANES)], axis=1).astype(BF16)
    merged = a_gated + b_gate * _dot(bo, wbs_ref[...]) + c_gated
    o_ref[0] = x + _rmsnorm(_dot(merged.astype(BF16), wo_ref[...]), post_ref[...])


def _mix(x, q, kt, v, layer, *stacked):
    b, s, d = x.shape
    t = min(SEQ_TILE, s)
    blk = ATT_BLOCK
    assert s % t == 0 and t % blk == 0
    nb = s // blk
    return pl.pallas_call(
        _mix_kernel,
        grid=(b, s // t),
        in_specs=[
            pl.BlockSpec((1, t, d), lambda bi, i: (bi, i, 0)),
            pl.BlockSpec((1, t, SB_WIDTH), lambda bi, i: (bi, i, 0)),
            pl.BlockSpec((1, nb, SB_WIDTH, blk), lambda bi, i: (bi, 0, 0, 0)),
            pl.BlockSpec((1, s, SB_WIDTH), lambda bi, i: (bi, 0, 0)),
        ] + [_layer_block(a, layer) for a in stacked],
        out_specs=pl.BlockSpec((1, t, d), lambda bi, i: (bi, i, 0)),
        out_shape=jax.ShapeDtypeStruct((b, s, d), F32),
        scratch_shapes=[pltpu.VMEM((POOL_HALO, POOL_WIDTH), F32),
                        pltpu.VMEM((CONV_HALO, CONV_WIDTH), F32),
                        pltpu.VMEM((SB_WIDTH // LANES, t, LANES), F32),
                        pltpu.VMEM((t // blk * SB_HEADS, blk, 1), F32)],
        compiler_params=pltpu.CompilerParams(
            dimension_semantics=("arbitrary", "arbitrary"), vmem_limit_bytes=VMEM_LIMIT),
        name="mix",
    )(x, q.reshape(b, s, SB_WIDTH), kt.reshape(b, nb, SB_WIDTH, blk), v.reshape(b, s, SB_WIDTH),
      *stacked)


def kernel(x, ffn1_pre_g, ffn1_post_g, ffn1_w_gate, ffn1_w_up, ffn1_w_down, mix_pre_g, mix_post_g,
           w_in, b_gate, pool_w, pool_scale, conv_w, conv_b, w_br_pool, w_br_sb, w_br_conv, w_out,
           ffn2_pre_g, ffn2_post_g, ffn2_w_gate, ffn2_w_up, ffn2_w_down):
    b, s, d = x.shape
    depth = w_in.shape[0]
    rows = lambda a: a.reshape(depth, 1, -1)
    bf = lambda a: a.astype(BF16)
    groups = len(POOL_WINDOWS)
    pool_bd = jnp.einsum("lgcd,gh->lgchd", pool_w, jnp.eye(groups, dtype=pool_w.dtype))
    pool_bd = pool_bd.reshape(depth, POOL_WIDTH, POOL_WIDTH)
    ffn1 = (rows(ffn1_pre_g), rows(ffn1_post_g), bf(ffn1_w_gate), bf(ffn1_w_up), bf(ffn1_w_down))
    ffn2 = (rows(ffn2_pre_g), rows(ffn2_post_g), bf(ffn2_w_gate), bf(ffn2_w_up), bf(ffn2_w_down))
    w_in_bf = bf(w_in)
    mix = (rows(mix_pre_g), rows(mix_post_g), w_in_bf, rows(b_gate), bf(pool_bd), rows(pool_scale),
           conv_w, rows(conv_b), bf(w_br_pool), bf(w_br_sb), bf(w_br_conv), bf(w_out))
    x = x.reshape(b * s, d)
    for l in range(depth):
        x = _ffn(x, l, *ffn1)
        q, kt, v = _qkv(x, l, mix[0], w_in_bf)
        x = _mix(x.reshape(b, s, d), q, kt, v, l, *mix).reshape(b * s, d)
        x = _ffn(x, l, *ffn2)
    return x.reshape(b, s, d)
```

```python
import functools

import jax
import jax.numpy as jnp
from jax import lax
from jax.experimental import pallas as pl
from jax.experimental.pallas import tpu as pltpu

F32 = jnp.float32
BF16 = jnp.bfloat16

EPS = 1e-6
D_MODEL = 1024
D_FF = 2816
FF_CHUNK = 256
POOL_WINDOWS = (2, 4, 8, 16)
POOL_GROUP_DIM = 64
POOL_WIDTH = 256
SB_HEADS = 8
SB_HEAD_DIM = 64
SB_WIDTH = 512
CONV_WIDTH = 256
N_BRANCH = 3
OFF_Q = POOL_WIDTH
OFF_CX = OFF_Q + 3 * SB_WIDTH
OFF_GATE = OFF_CX + 3 * CONV_WIDTH
LANES = 128
POOL_HALO = 16
CONV_HALO = 8

FFN_ROWS = 1024
QKV_ROWS = 1024
SEQ_TILE = 512
ATT_BLOCK = 256
STAGE_SKEW = 1
P_CHUNK = 256
VMEM_LIMIT = 56 * 1024 * 1024
LIVE_MIN = -105.0
DEAD = -1e30


def _rmsnorm(x, g):
    ms = jnp.mean(x * x, axis=-1, keepdims=True)
    return x * lax.rsqrt(ms + EPS) * g


def _sigmoid(x):
    return 1.0 / (1.0 + jnp.exp(-x))


def _dot(a, b):
    return jnp.dot(a, b, preferred_element_type=F32)


def _layer_block(stacked, layer):
    tail = stacked.shape[1:]
    return pl.BlockSpec((None,) + tail, lambda *_: (layer,) + (0,) * len(tail),
                        pipeline_mode=pl.Buffered(1))


def _ffn_kernel(x_ref, pre_ref, post_ref, wg_ref, wu_ref, wd_ref, o_ref, a_scr):
    half = x_ref.shape[0] // 2
    parts = [slice(0, half), slice(half, 2 * half)]
    xs = [x_ref[r, :] for r in parts]
    hs = [_rmsnorm(x, pre_ref[...]).astype(BF16) for x in xs]
    for r, h in zip(parts, hs):
        for c in range(D_FF // FF_CHUNK):
            cols = slice(c * FF_CHUNK, (c + 1) * FF_CHUNK)
            g = _dot(h, wg_ref[:, cols])
            u = _dot(h, wu_ref[:, cols])
            a_scr[r, cols] = ((g * _sigmoid(g)) * u).astype(BF16)
    ys = [_dot(a_scr[r, :], wd_ref[...]) for r in parts]
    for r, x, y in zip(parts, xs, ys):
        o_ref[r, :] = x + 0.5 * _rmsnorm(y, post_ref[...])


def _ffn(x2d, layer, pre_g, post_g, wg, wu, wd):
    n, d = x2d.shape
    rows = min(FFN_ROWS, n)
    assert n % rows == 0
    return pl.pallas_call(
        _ffn_kernel,
        grid=(n // rows,),
        in_specs=[pl.BlockSpec((rows, d), lambda i: (i, 0))]
        + [_layer_block(a, layer) for a in (pre_g, post_g, wg, wu, wd)],
        out_specs=pl.BlockSpec((rows, d), lambda i: (i, 0)),
        out_shape=jax.ShapeDtypeStruct((n, d), F32),
        scratch_shapes=[pltpu.VMEM((rows, D_FF), BF16)],
        compiler_params=pltpu.CompilerParams(
            dimension_semantics=("arbitrary",), vmem_limit_bytes=VMEM_LIMIT),
        name="ffn",
    )(x2d, pre_g, post_g, wg, wu, wd)


def _qkv_kernel(x_ref, g_ref, w_ref, q_ref, kt_ref, v_ref):
    h = _rmsnorm(x_ref[...], g_ref[...]).astype(BF16)
    qkv = _dot(h, w_ref[:, OFF_Q:OFF_CX])
    q_ref[...] = (qkv[:, :SB_WIDTH] * (SB_HEAD_DIM ** -0.5)).astype(BF16)
    for c in range(kt_ref.shape[0]):
        k = qkv[c * ATT_BLOCK:(c + 1) * ATT_BLOCK, SB_WIDTH:2 * SB_WIDTH]
        kt_ref[c] = k.T.astype(BF16)
    v_ref[...] = qkv[:, 2 * SB_WIDTH:].astype(BF16)


def _qkv(x2d, layer, g, w_in):
    n, d = x2d.shape
    rows = min(QKV_ROWS, n)
    assert n % rows == 0 and rows % ATT_BLOCK == 0
    nblk = rows // ATT_BLOCK
    return pl.pallas_call(
        _qkv_kernel,
        grid=(n // rows,),
        in_specs=[pl.BlockSpec((rows, d), lambda i: (i, 0)),
                  _layer_block(g, layer), _layer_block(w_in, layer)],
        out_specs=[
            pl.BlockSpec((rows, SB_WIDTH), lambda i: (i, 0)),
            pl.BlockSpec((nblk, SB_WIDTH, ATT_BLOCK), lambda i: (i, 0, 0)),
            pl.BlockSpec((rows, SB_WIDTH), lambda i: (i, 0)),
        ],
        out_shape=[
            jax.ShapeDtypeStruct((n, SB_WIDTH), BF16),
            jax.ShapeDtypeStruct((n // ATT_BLOCK, SB_WIDTH, ATT_BLOCK), BF16),
            jax.ShapeDtypeStruct((n, SB_WIDTH), BF16),
        ],
        compiler_params=pltpu.CompilerParams(
            dimension_semantics=("arbitrary",), vmem_limit_bytes=VMEM_LIMIT),
        name="qkv",
    )(x2d, g, w_in)


class _Sweep:
    def __init__(self, q_ref, kt_ref, v_ref, acc_scr, later_scr):
        self.q_ref, self.kt_ref, self.v_ref = q_ref, kt_ref, v_ref
        self.acc_scr, self.later_scr = acc_scr, later_scr
        blk = ATT_BLOCK
        row = lax.broadcasted_iota(jnp.int32, (blk, blk), 0)
        col = lax.broadcasted_iota(jnp.int32, (blk, blk), 1)
        lower = row > col
        ntri = jnp.where(lower, -1.0, 0.0).astype(BF16)
        self.ntri2 = jnp.concatenate([ntri, ntri], axis=0)
        self.lower_half = lower[:blk // 2, :blk // 2]
        lane = lax.broadcasted_iota(jnp.int32, (1, LANES), 1)
        self.head_lanes = (lane < SB_HEAD_DIM, lane >= SB_HEAD_DIM)

    @staticmethod
    def _split(x, diagonal):
        half = ATT_BLOCK // 2
        return (x[:half, :half], x[half:, :half], x[half:, half:]) if diagonal else (x,)

    @staticmethod
    def _join(pieces, diagonal):
        if not diagonal:
            return pieces[0]
        tl, bl, br = pieces
        top = jnp.concatenate([tl, jnp.zeros_like(tl)], axis=1)
        return jnp.concatenate([top, jnp.concatenate([bl, br], axis=1)], axis=0)

    @staticmethod
    def _rows(r):
        if isinstance(r, int):
            return slice(r * ATT_BLOCK, (r + 1) * ATT_BLOCK)
        return pl.ds(pl.multiple_of(r * ATT_BLOCK, ATT_BLOCK), ATT_BLOCK)

    def _masks(self, diagonal):
        return (self.lower_half, None, self.lower_half) if diagonal else (None,)

    def scores(self, r, h, j, diagonal, guard):
        lanes = slice((h // 2) * LANES, (h // 2 + 1) * LANES)
        qp = self.q_ref[0, self._rows(r), lanes]
        qm = jnp.where(self.head_lanes[h % 2], qp, jnp.zeros_like(qp))
        z = _dot(qm, self.kt_ref[0, j, lanes, :])
        later = None if diagonal else self.later_scr[r * SB_HEADS + h]
        if guard is not None:
            later = jnp.where(guard, later, DEAD)
        ps, his, los, lbs = [], [], [], []
        for zq, mask in zip(self._split(z, diagonal), self._masks(diagonal)):
            neg_abs = lax.bitcast_convert_type(
                lax.bitcast_convert_type(zq, jnp.uint32) | jnp.uint32(0x80000000), F32)
            p = jnp.maximum(zq, 0.0) + jnp.log(1.0 + jnp.exp(neg_abs))
            if mask is not None:
                p = jnp.where(mask, p, 0.0)
            hi = p.astype(BF16)
            ps.append(p)
            his.append(hi)
            los.append((p - hi.astype(F32)).astype(BF16))
            lbs.append(zq - p if diagonal else (zq - p) + later)
        spent = jnp.sum(self._join(ps, diagonal), axis=1, keepdims=True)
        later = -spent if diagonal else later - spent
        self.later_scr[r * SB_HEADS + h] = later
        hl = jnp.concatenate([self._join(his, diagonal), self._join(los, diagonal)], axis=1)
        return (hl, lbs), later

    def weights(self, hl, lbs, diagonal):
        ws = []
        for lb, between, mask in zip(lbs, self._split(_dot(hl, self.ntri2), diagonal),
                                     self._masks(diagonal)):
            w = jnp.exp(lb + between)
            if mask is not None:
                w = jnp.where(mask, w, 0.0)
            ws.append(w.astype(BF16))
        return self._join(ws, diagonal)

    def values(self, r, h, j, diagonal, w):
        blk = ATT_BLOCK
        lanes = slice((h // 2) * LANES, (h // 2 + 1) * LANES)
        vp = self.v_ref[0, pl.ds(pl.multiple_of(j * blk, blk), blk), lanes]
        vm = jnp.where(self.head_lanes[h % 2], vp, jnp.zeros_like(vp))
        rows = self._rows(r)
        if diagonal and h % 2 == 0:
            self.acc_scr[h // 2, rows, :] = _dot(w, vm)
        else:
            self.acc_scr[h // 2, rows, :] += _dot(w, vm)

    def run(self, groups, fillers=()):
        units = [(g, r, h, j, diagonal, guard) for g, (r, j, diagonal, guard) in enumerate(groups)
                 for h in range(SB_HEADS)]
        fillers, filled = list(fillers), []
        stage1, stage2, latest = {}, {}, {}
        for n in range(len(units) + 2 * STAGE_SKEW):
            if n < len(units):
                g, r, h, j, diagonal, guard = units[n]
                stage1[n], later = self.scores(r, h, j, diagonal, guard)
                latest[g] = later if h == 0 else jnp.maximum(latest[g], later)
            while len(filled) < min(len(fillers), (n + 1) * len(fillers) // len(units)):
                filled.append(fillers[len(filled)]())
            m = n - STAGE_SKEW
            if 0 <= m < len(units):
                stage2[m] = self.weights(*stage1.pop(m), units[m][4])
            m = n - 2 * STAGE_SKEW
            if 0 <= m:
                _, r, h, j, diagonal, _ = units[m]
                self.values(r, h, j, diagonal, stage2.pop(m))
        live = [(jnp.max(latest[g]) > LIVE_MIN).astype(jnp.int32) for g in range(len(groups))]
        return live, filled


def _mix_kernel(x_ref, q_ref, kt_ref, v_ref, pre_ref, post_ref, w_ref, bg_ref, pbd_ref, ps_ref,
                cw_ref, cb_ref, wbp_ref, wbs_ref, wbc_ref, wo_ref, o_ref,
                pool_halo, conv_halo, acc_scr, later_scr):
    t = x_ref.shape[1]
    i = pl.program_id(1)
    nq = t // ATT_BLOCK

    @pl.when(i == 0)
    def _():
        pool_halo[...] = jnp.zeros_like(pool_halo)
        conv_halo[...] = jnp.zeros_like(conv_halo)

    lane = lax.broadcasted_iota(jnp.int32, (1, POOL_WIDTH), 1)
    x = x_ref[0]
    h = _rmsnorm(x, pre_ref[...]).astype(BF16)
    u = _dot(h, w_ref[:, :OFF_Q])

    sweep = _Sweep(q_ref, kt_ref, v_ref, acc_scr, later_scr)
    groups = [(r, nq * i + r, True, None) for r in range(nq)]
    for r in range(nq):
        guard = (i > 0) if r == 0 else None
        groups.append((r, jnp.maximum(nq * i + r - 1, 0), False, guard))
    n_chunks = (w_ref.shape[1] - OFF_CX) // P_CHUNK
    chunks = [functools.partial(
        lambda c: _dot(h, w_ref[:, OFF_CX + c * P_CHUNK:OFF_CX + (c + 1) * P_CHUNK]), c)
        for c in range(n_chunks)]
    live, p_chunks = sweep.run(groups, chunks)
    p = jnp.concatenate(p_chunks, axis=1)

    ext = jnp.concatenate([pool_halo[...], u], axis=0)
    win, width = ext, jnp.zeros((1, POOL_WIDTH), F32)
    acc = ext
    for g, wnd in enumerate(POOL_WINDOWS):
        acc = acc + pltpu.roll(acc, wnd // 2, 0)
        in_group = (lane >= g * POOL_GROUP_DIM) & (lane < (g + 1) * POOL_GROUP_DIM)
        win = jnp.where(in_group, acc, win)
        width = jnp.where(in_group, float(wnd), width)
    pos = (i * t + 1 + lax.broadcasted_iota(jnp.int32, (t, 1), 0)).astype(F32)
    pooled = win[POOL_HALO:] / jnp.minimum(pos, width) - u
    pool_halo[...] = u[t - POOL_HALO:]
    a_out = _dot(pooled.astype(BF16), pbd_ref[...]) * ps_ref[...]

    xc = p[:, :CONV_WIDTH]
    gb = p[:, CONV_WIDTH:2 * CONV_WIDTH]
    gc = p[:, 2 * CONV_WIDTH:3 * CONV_WIDTH]
    uc = gc * xc
    extc = jnp.concatenate([conv_halo[...], uc], axis=0)
    y = (cw_ref[0:1, :] * pltpu.roll(extc, 2, 0) + cw_ref[1:2, :] * pltpu.roll(extc, 1, 0)
         + cw_ref[2:3, :] * extc)
    c_out = gb * (cb_ref[...] + y[CONV_HALO:])
    conv_halo[...] = uc[t - CONV_HALO:]

    gates = _sigmoid(p[:, 3 * CONV_WIDTH:] + bg_ref[...])
    a_gated = gates[:, :D_MODEL] * _dot(a_out.astype(BF16), wbp_ref[...])
    c_gated = gates[:, 2 * D_MODEL:] * _dot(c_out.astype(BF16), wbc_ref[...])
    b_gate = gates[:, D_MODEL:2 * D_MODEL]

    def pending(s):
        return [(s[2 * r] >= 0) & (s[2 * r + 1] > 0) for r in range(nq)]

    def body(s):
        todo = pending(s)
        r, j = jnp.int32(nq - 1), s[2 * (nq - 1)]
        for k in reversed(range(nq - 1)):
            r, j = jnp.where(todo[k], k, r), jnp.where(todo[k], s[2 * k], j)
        (now,), _ = sweep.run([(r, j, False, None)])
        out = []
        for k in range(nq):
            mine = r == k
            out += [jnp.where(mine, s[2 * k] - 1, s[2 * k]), jnp.where(mine, now, s[2 * k + 1])]
        return tuple(out)

    start = []
    for r in range(nq):
        start += [nq * i + r - 2, live[nq + r]]
    lax.while_loop(lambda s: functools.reduce(jnp.logical_or, pending(s)), body, tuple(start))

    bo = jnp.concatenate([acc_scr[hp] for hp in range(SB_WIDTH // LANES)], axis=1).astype(BF16)
    merged = a_gated + b_gate * _dot(bo, wbs_ref[...]) + c_gated
    o_ref[0] = x + _rmsnorm(_dot(merged.astype(BF16), wo_ref[...]), post_ref[...])


def _mix(x, q, kt, v, layer, *stacked):
    b, s, d = x.shape
    t = min(SEQ_TILE, s)
    blk = ATT_BLOCK
    assert s % t == 0 and t % blk == 0
    nb = s // blk
    return pl.pallas_call(
        _mix_kernel,
        grid=(b, s // t),
        in_specs=[
            pl.BlockSpec((1, t, d), lambda bi, i: (bi, i, 0)),
            pl.BlockSpec((1, t, SB_WIDTH), lambda bi, i: (bi, i, 0)),
            pl.BlockSpec((1, nb, SB_WIDTH, blk), lambda bi, i: (bi, 0, 0, 0)),
            pl.BlockSpec((1, s, SB_WIDTH), lambda bi, i: (bi, 0, 0)),
        ] + [_layer_block(a, layer) for a in stacked],
        out_specs=pl.BlockSpec((1, t, d), lambda bi, i: (bi, i, 0)),
        out_shape=jax.ShapeDtypeStruct((b, s, d), F32),
        scratch_shapes=[pltpu.VMEM((POOL_HALO, POOL_WIDTH), F32),
                        pltpu.VMEM((CONV_HALO, CONV_WIDTH), F32),
                        pltpu.VMEM((SB_WIDTH // LANES, t, LANES), F32),
                        pltpu.VMEM((t // blk * SB_HEADS, blk, 1), F32)],
        compiler_params=pltpu.CompilerParams(
            dimension_semantics=("arbitrary", "arbitrary"), vmem_limit_bytes=VMEM_LIMIT),
        name="mix",
    )(x, q.reshape(b, s, SB_WIDTH), kt.reshape(b, nb, SB_WIDTH, blk), v.reshape(b, s, SB_WIDTH),
      *stacked)


def kernel(x, ffn1_pre_g, ffn1_post_g, ffn1_w_gate, ffn1_w_up, ffn1_w_down, mix_pre_g, mix_post_g,
           w_in, b_gate, pool_w, pool_scale, conv_w, conv_b, w_br_pool, w_br_sb, w_br_conv, w_out,
           ffn2_pre_g, ffn2_post_g, ffn2_w_gate, ffn2_w_up, ffn2_w_down):
    b, s, d = x.shape
    depth = w_in.shape[0]
    rows = lambda a: a.reshape(depth, 1, -1)
    bf = lambda a: a.astype(BF16)
    groups = len(POOL_WINDOWS)
    pool_bd = jnp.einsum("lgcd,gh->lgchd", pool_w, jnp.eye(groups, dtype=pool_w.dtype))
    pool_bd = pool_bd.reshape(depth, POOL_WIDTH, POOL_WIDTH)
    ffn1 = (rows(ffn1_pre_g), rows(ffn1_post_g), bf(ffn1_w_gate), bf(ffn1_w_up), bf(ffn1_w_down))
    ffn2 = (rows(ffn2_pre_g), rows(ffn2_post_g), bf(ffn2_w_gate), bf(ffn2_w_up), bf(ffn2_w_down))
    w_in_bf = bf(w_in)
    mix = (rows(mix_pre_g), rows(mix_post_g), w_in_bf, rows(b_gate), bf(pool_bd), rows(pool_scale),
           conv_w, rows(conv_b), bf(w_br_pool), bf(w_br_sb), bf(w_br_conv), bf(w_out))
    x = x.reshape(b * s, d)
    for l in range(depth):
        x = _ffn(x, l, *ffn1)
        q, kt, v = _qkv(x, l, mix[0], w_in_bf)
        x = _mix(x.reshape(b, s, d), q, kt, v, l, *mix).reshape(b * s, d)
        x = _ffn(x, l, *ffn2)
    return x.reshape(b, s, d)
```

```python
import functools

import jax
import jax.numpy as jnp
from jax import lax
from jax.experimental import pallas as pl
from jax.experimental.pallas import tpu as pltpu

F32 = jnp.float32
BF16 = jnp.bfloat16

EPS = 1e-6
D_MODEL = 1024
D_FF = 2816
FF_CHUNK = 256
POOL_WINDOWS = (2, 4, 8, 16)
POOL_GROUP_DIM = 64
POOL_WIDTH = 256
SB_HEADS = 8
SB_HEAD_DIM = 64
SB_WIDTH = 512
CONV_WIDTH = 256
N_BRANCH = 3
OFF_Q = POOL_WIDTH
OFF_CX = OFF_Q + 3 * SB_WIDTH
OFF_GATE = OFF_CX + 3 * CONV_WIDTH
LANES = 128
POOL_HALO = 16
CONV_HALO = 8

FFN_ROWS = 1024
FFN_PARTS = 4
QKV_ROWS = 1024
SEQ_TILE = 512
ATT_BLOCK = 256
STAGE_SKEW = 1
P_CHUNK = 256
VMEM_LIMIT = 56 * 1024 * 1024
LIVE_MIN = -105.0
DEAD = -1e30


def _rmsnorm(x, g):
    ms = jnp.mean(x * x, axis=-1, keepdims=True)
    return x * lax.rsqrt(ms + EPS) * g


def _sigmoid(x):
    return 1.0 / (1.0 + jnp.exp(-x))


def _dot(a, b):
    return jnp.dot(a, b, preferred_element_type=F32)


def _layer_block(stacked, layer):
    tail = stacked.shape[1:]
    return pl.BlockSpec((None,) + tail, lambda *_: (layer,) + (0,) * len(tail),
                        pipeline_mode=pl.Buffered(1))


def _ffn_kernel(x_ref, pre_ref, post_ref, wg_ref, wu_ref, wd_ref, o_ref, a_scr):
    rows = x_ref.shape[0] // FFN_PARTS
    parts = [slice(k * rows, (k + 1) * rows) for k in range(FFN_PARTS)]
    xs = [x_ref[r, :] for r in parts]
    hs = [_rmsnorm(x, pre_ref[...]).astype(BF16) for x in xs]
    for r, h in zip(parts, hs):
        for c in range(D_FF // FF_CHUNK):
            cols = slice(c * FF_CHUNK, (c + 1) * FF_CHUNK)
            g = _dot(h, wg_ref[:, cols])
            u = _dot(h, wu_ref[:, cols])
            a_scr[r, cols] = ((g * _sigmoid(g)) * u).astype(BF16)
    ys = [_dot(a_scr[r, :], wd_ref[...]) for r in parts]
    for r, x, y in zip(parts, xs, ys):
        o_ref[r, :] = x + 0.5 * _rmsnorm(y, post_ref[...])


def _ffn(x2d, layer, pre_g, post_g, wg, wu, wd):
    n, d = x2d.shape
    rows = min(FFN_ROWS, n)
    assert n % rows == 0
    return pl.pallas_call(
        _ffn_kernel,
        grid=(n // rows,),
        in_specs=[pl.BlockSpec((rows, d), lambda i: (i, 0))]
        + [_layer_block(a, layer) for a in (pre_g, post_g, wg, wu, wd)],
        out_specs=pl.BlockSpec((rows, d), lambda i: (i, 0)),
        out_shape=jax.ShapeDtypeStruct((n, d), F32),
        scratch_shapes=[pltpu.VMEM((rows, D_FF), BF16)],
        compiler_params=pltpu.CompilerParams(
            dimension_semantics=("arbitrary",), vmem_limit_bytes=VMEM_LIMIT),
        name="ffn",
    )(x2d, pre_g, post_g, wg, wu, wd)


def _qkv_kernel(x_ref, g_ref, w_ref, q_ref, kt_ref, v_ref):
    h = _rmsnorm(x_ref[...], g_ref[...]).astype(BF16)
    qkv = _dot(h, w_ref[:, OFF_Q:OFF_CX])
    q_ref[...] = (qkv[:, :SB_WIDTH] * (SB_HEAD_DIM ** -0.5)).astype(BF16)
    for c in range(kt_ref.shape[0]):
        k = qkv[c * ATT_BLOCK:(c + 1) * ATT_BLOCK, SB_WIDTH:2 * SB_WIDTH]
        kt_ref[c] = k.T.astype(BF16)
    v_ref[...] = qkv[:, 2 * SB_WIDTH:].astype(BF16)


def _qkv(x2d, layer, g, w_in):
    n, d = x2d.shape
    rows = min(QKV_ROWS, n)
    assert n % rows == 0 and rows % ATT_BLOCK == 0
    nblk = rows // ATT_BLOCK
    return pl.pallas_call(
        _qkv_kernel,
        grid=(n // rows,),
        in_specs=[pl.BlockSpec((rows, d), lambda i: (i, 0)),
                  _layer_block(g, layer), _layer_block(w_in, layer)],
        out_specs=[
            pl.BlockSpec((rows, SB_WIDTH), lambda i: (i, 0)),
            pl.BlockSpec((nblk, SB_WIDTH, ATT_BLOCK), lambda i: (i, 0, 0)),
            pl.BlockSpec((rows, SB_WIDTH), lambda i: (i, 0)),
        ],
        out_shape=[
            jax.ShapeDtypeStruct((n, SB_WIDTH), BF16),
            jax.ShapeDtypeStruct((n // ATT_BLOCK, SB_WIDTH, ATT_BLOCK), BF16),
            jax.ShapeDtypeStruct((n, SB_WIDTH), BF16),
        ],
        compiler_params=pltpu.CompilerParams(
            dimension_semantics=("arbitrary",), vmem_limit_bytes=VMEM_LIMIT),
        name="qkv",
    )(x2d, g, w_in)


class _Sweep:
    def __init__(self, q_ref, kt_ref, v_ref, acc_scr, later_scr):
        self.q_ref, self.kt_ref, self.v_ref = q_ref, kt_ref, v_ref
        self.acc_scr, self.later_scr = acc_scr, later_scr
        blk = ATT_BLOCK
        row = lax.broadcasted_iota(jnp.int32, (blk, blk), 0)
        col = lax.broadcasted_iota(jnp.int32, (blk, blk), 1)
        lower = row > col
        ntri = jnp.where(lower, -1.0, 0.0).astype(BF16)
        self.ntri2 = jnp.concatenate([ntri, ntri], axis=0)
        self.lower_half = lower[:blk // 2, :blk // 2]
        lane = lax.broadcasted_iota(jnp.int32, (1, LANES), 1)
        self.head_lanes = (lane < SB_HEAD_DIM, lane >= SB_HEAD_DIM)

    @staticmethod
    def _split(x, diagonal):
        half = ATT_BLOCK // 2
        return (x[:half, :half], x[half:, :half], x[half:, half:]) if diagonal else (x,)

    @staticmethod
    def _join(pieces, diagonal):
        if not diagonal:
            return pieces[0]
        tl, bl, br = pieces
        top = jnp.concatenate([tl, jnp.zeros_like(tl)], axis=1)
        return jnp.concatenate([top, jnp.concatenate([bl, br], axis=1)], axis=0)

    @staticmethod
    def _rows(r):
        if isinstance(r, int):
            return slice(r * ATT_BLOCK, (r + 1) * ATT_BLOCK)
        return pl.ds(pl.multiple_of(r * ATT_BLOCK, ATT_BLOCK), ATT_BLOCK)

    def _masks(self, diagonal):
        return (self.lower_half, None, self.lower_half) if diagonal else (None,)

    def scores(self, r, h, j, diagonal, guard):
        lanes = slice((h // 2) * LANES, (h // 2 + 1) * LANES)
        qp = self.q_ref[0, self._rows(r), lanes]
        qm = jnp.where(self.head_lanes[h % 2], qp, jnp.zeros_like(qp))
        z = _dot(qm, self.kt_ref[0, j, lanes, :])
        later = None if diagonal else self.later_scr[r * SB_HEADS + h]
        if guard is not None:
            later = jnp.where(guard, later, DEAD)
        ps, his, los, lbs = [], [], [], []
        for zq, mask in zip(self._split(z, diagonal), self._masks(diagonal)):
            neg_abs = lax.bitcast_convert_type(
                lax.bitcast_convert_type(zq, jnp.uint32) | jnp.uint32(0x80000000), F32)
            p = jnp.maximum(zq, 0.0) + jnp.log(1.0 + jnp.exp(neg_abs))
            if mask is not None:
                p = jnp.where(mask, p, 0.0)
            hi = p.astype(BF16)
            ps.append(p)
            his.append(hi)
            los.append((p - hi.astype(F32)).astype(BF16))
            lbs.append(zq - p if diagonal else (zq - p) + later)
        spent = jnp.sum(self._join(ps, diagonal), axis=1, keepdims=True)
        later = -spent if diagonal else later - spent
        self.later_scr[r * SB_HEADS + h] = later
        hl = jnp.concatenate([self._join(his, diagonal), self._join(los, diagonal)], axis=1)
        return (hl, lbs), later

    def weights(self, hl, lbs, diagonal):
        ws = []
        for lb, between, mask in zip(lbs, self._split(_dot(hl, self.ntri2), diagonal),
                                     self._masks(diagonal)):
            w = jnp.exp(lb + between)
            if mask is not None:
                w = jnp.where(mask, w, 0.0)
            ws.append(w.astype(BF16))
        return self._join(ws, diagonal)

    def values(self, r, h, j, diagonal, w):
        blk = ATT_BLOCK
        lanes = slice((h // 2) * LANES, (h // 2 + 1) * LANES)
        vp = self.v_ref[0, pl.ds(pl.multiple_of(j * blk, blk), blk), lanes]
        vm = jnp.where(self.head_lanes[h % 2], vp, jnp.zeros_like(vp))
        rows = self._rows(r)
        if diagonal and h % 2 == 0:
            self.acc_scr[h // 2, rows, :] = _dot(w, vm)
        else:
            self.acc_scr[h // 2, rows, :] += _dot(w, vm)

    def run(self, groups, fillers=()):
        units = [(g, r, h, j, diagonal, guard) for g, (r, j, diagonal, guard) in enumerate(groups)
                 for h in range(SB_HEADS)]
        fillers, filled = list(fillers), []
        stage1, stage2, latest = {}, {}, {}
        for n in range(len(units) + 2 * STAGE_SKEW):
            if n < len(units):
                g, r, h, j, diagonal, guard = units[n]
                stage1[n], later = self.scores(r, h, j, diagonal, guard)
                latest[g] = later if h == 0 else jnp.maximum(latest[g], later)
            while len(filled) < min(len(fillers), (n + 1) * len(fillers) // len(units)):
                filled.append(fillers[len(filled)]())
            m = n - STAGE_SKEW
            if 0 <= m < len(units):
                stage2[m] = self.weights(*stage1.pop(m), units[m][4])
            m = n - 2 * STAGE_SKEW
            if 0 <= m:
                _, r, h, j, diagonal, _ = units[m]
                self.values(r, h, j, diagonal, stage2.pop(m))
        live = [(jnp.max(latest[g]) > LIVE_MIN).astype(jnp.int32) for g in range(len(groups))]
        return live, filled


def _mix_kernel(x_ref, q_ref, kt_ref, v_ref, pre_ref, post_ref, w_ref, bg_ref, pbd_ref, ps_ref,
                cw_ref, cb_ref, wbp_ref, wbs_ref, wbc_ref, wo_ref, o_ref,
                pool_halo, conv_halo, acc_scr, later_scr):
    t = x_ref.shape[1]
    i = pl.program_id(1)
    nq = t // ATT_BLOCK

    @pl.when(i == 0)
    def _():
        pool_halo[...] = jnp.zeros_like(pool_halo)
        conv_halo[...] = jnp.zeros_like(conv_halo)

    lane = lax.broadcasted_iota(jnp.int32, (1, POOL_WIDTH), 1)
    x = x_ref[0]
    h = _rmsnorm(x, pre_ref[...]).astype(BF16)
    u = _dot(h, w_ref[:, :OFF_Q])

    sweep = _Sweep(q_ref, kt_ref, v_ref, acc_scr, later_scr)
    groups = [(r, nq * i + r, True, None) for r in range(nq)]
    for r in range(nq):
        guard = (i > 0) if r == 0 else None
        groups.append((r, jnp.maximum(nq * i + r - 1, 0), False, guard))
    n_chunks = (w_ref.shape[1] - OFF_CX) // P_CHUNK
    chunks = [functools.partial(
        lambda c: _dot(h, w_ref[:, OFF_CX + c * P_CHUNK:OFF_CX + (c + 1) * P_CHUNK]), c)
        for c in range(n_chunks)]
    live, p_chunks = sweep.run(groups, chunks)
    p = jnp.concatenate(p_chunks, axis=1)

    ext = jnp.concatenate([pool_halo[...], u], axis=0)
    win, width = ext, jnp.zeros((1, POOL_WIDTH), F32)
    acc = ext
    for g, wnd in enumerate(POOL_WINDOWS):
        acc = acc + pltpu.roll(acc, wnd // 2, 0)
        in_group = (lane >= g * POOL_GROUP_DIM) & (lane < (g + 1) * POOL_GROUP_DIM)
        win = jnp.where(in_group, acc, win)
        width = jnp.where(in_group, float(wnd), width)
    pos = (i * t + 1 + lax.broadcasted_iota(jnp.int32, (t, 1), 0)).astype(F32)
    pooled = win[POOL_HALO:] / jnp.minimum(pos, width) - u
    pool_halo[...] = u[t - POOL_HALO:]
    a_out = _dot(pooled.astype(BF16), pbd_ref[...]) * ps_ref[...]

    xc = p[:, :CONV_WIDTH]
    gb = p[:, CONV_WIDTH:2 * CONV_WIDTH]
    gc = p[:, 2 * CONV_WIDTH:3 * CONV_WIDTH]
    uc = gc * xc
    extc = jnp.concatenate([conv_halo[...], uc], axis=0)
    y = (cw_ref[0:1, :] * pltpu.roll(extc, 2, 0) + cw_ref[1:2, :] * pltpu.roll(extc, 1, 0)
         + cw_ref[2:3, :] * extc)
    c_out = gb * (cb_ref[...] + y[CONV_HALO:])
    conv_halo[...] = uc[t - CONV_HALO:]

    gates = _sigmoid(p[:, 3 * CONV_WIDTH:] + bg_ref[...])
    a_gated = gates[:, :D_MODEL] * _dot(a_out.astype(BF16), wbp_ref[...])
    c_gated = gates[:, 2 * D_MODEL:] * _dot(c_out.astype(BF16), wbc_ref[...])
    b_gate = gates[:, D_MODEL:2 * D_MODEL]

    def pending(s):
        return [(s[2 * r] >= 0) & (s[2 * r + 1] > 0) for r in range(nq)]

    def body(s):
        todo = pending(s)
        r, j = jnp.int32(nq - 1), s[2 * (nq - 1)]
        for k in reversed(range(nq - 1)):
            r, j = jnp.where(todo[k], k, r), jnp.where(todo[k], s[2 * k], j)
        (now,), _ = sweep.run([(r, j, False, None)])
        out = []
        for k in range(nq):
            mine = r == k
            out += [jnp.where(mine, s[2 * k] - 1, s[2 * k]), jnp.where(mine, now, s[2 * k + 1])]
        return tuple(out)

    start = []
    for r in range(nq):
        start += [nq * i + r - 2, live[nq + r]]
    lax.while_loop(lambda s: functools.reduce(jnp.logical_or, pending(s)), body, tuple(start))

    bo = jnp.concatenate([acc_scr[hp] for hp in range(SB_WIDTH // LANES)], axis=1).astype(BF16)
    merged = a_gated + b_gate * _dot(bo, wbs_ref[...]) + c_gated
    o_ref[0] = x + _rmsnorm(_dot(merged.astype(BF16), wo_ref[...]), post_ref[...])


def _mix(x, q, kt, v, layer, *stacked):
    b, s, d = x.shape
    t = min(SEQ_TILE, s)
    blk = ATT_BLOCK
    assert s % t == 0 and t % blk == 0
    nb = s // blk
    return pl.pallas_call(
        _mix_kernel,
        grid=(b, s // t),
        in_specs=[
            pl.BlockSpec((1, t, d), lambda bi, i: (bi, i, 0)),
            pl.BlockSpec((1, t, SB_WIDTH), lambda bi, i: (bi, i, 0)),
            pl.BlockSpec((1, nb, SB_WIDTH, blk), lambda bi, i: (bi, 0, 0, 0)),
            pl.BlockSpec((1, s, SB_WIDTH), lambda bi, i: (bi, 0, 0)),
        ] + [_layer_block(a, layer) for a in stacked],
        out_specs=pl.BlockSpec((1, t, d), lambda bi, i: (bi, i, 0)),
        out_shape=jax.ShapeDtypeStruct((b, s, d), F32),
        scratch_shapes=[pltpu.VMEM((POOL_HALO, POOL_WIDTH), F32),
                        pltpu.VMEM((CONV_HALO, CONV_WIDTH), F32),
                        pltpu.VMEM((SB_WIDTH // LANES, t, LANES), F32),
                        pltpu.VMEM((t // blk * SB_HEADS, blk, 1), F32)],
        compiler_params=pltpu.CompilerParams(
            dimension_semantics=("arbitrary", "arbitrary"), vmem_limit_bytes=VMEM_LIMIT),
        name="mix",
    )(x, q.reshape(b, s, SB_WIDTH), kt.reshape(b, nb, SB_WIDTH, blk), v.reshape(b, s, SB_WIDTH),
      *stacked)


def kernel(x, ffn1_pre_g, ffn1_post_g, ffn1_w_gate, ffn1_w_up, ffn1_w_down, mix_pre_g, mix_post_g,
           w_in, b_gate, pool_w, pool_scale, conv_w, conv_b, w_br_pool, w_br_sb, w_br_conv, w_out,
           ffn2_pre_g, ffn2_post_g, ffn2_w_gate, ffn2_w_up, ffn2_w_down):
    b, s, d = x.shape
    depth = w_in.shape[0]
    rows = lambda a: a.reshape(depth, 1, -1)
    bf = lambda a: a.astype(BF16)
    groups = len(POOL_WINDOWS)
    pool_bd = jnp.einsum("lgcd,gh->lgchd", pool_w, jnp.eye(groups, dtype=pool_w.dtype))
    pool_bd = pool_bd.reshape(depth, POOL_WIDTH, POOL_WIDTH)
    ffn1 = (rows(ffn1_pre_g), rows(ffn1_post_g), bf(ffn1_w_gate), bf(ffn1_w_up), bf(ffn1_w_down))
    ffn2 = (rows(ffn2_pre_g), rows(ffn2_post_g), bf(ffn2_w_gate), bf(ffn2_w_up), bf(ffn2_w_down))
    w_in_bf = bf(w_in)
    mix = (rows(mix_pre_g), rows(mix_post_g), w_in_bf, rows(b_gate), bf(pool_bd), rows(pool_scale),
           conv_w, rows(conv_b), bf(w_br_pool), bf(w_br_sb), bf(w_br_conv), bf(w_out))
    x = x.reshape(b * s, d)
    for l in range(depth):
        x = _ffn(x, l, *ffn1)
        q, kt, v = _qkv(x, l, mix[0], w_in_bf)
        x = _mix(x.reshape(b, s, d), q, kt, v, l, *mix).reshape(b * s, d)
        x = _ffn(x, l, *ffn2)
    return x.reshape(b, s, d)
```
